```python
import math
import jax
import jax.numpy as jnp
from jax import lax
import numpy as np

D_MODEL = 2048
BATCH = 16
SEQ = 256
DEPTH = 4
DEC_BATCH = 2
DEC_SEQ = 1024
PAST_LEN = 256

GRID_W = 64
DIFF_HEADS = 4
DIFF_HD = 64
DIFF_VD = 2 * DIFF_HD
MLA_HEADS = 8
MLA_Q_RANK = 512
MLA_KV_RANK = 256
MLA_NOPE = 128
MLA_ROPE = 64
MLA_VD = 128
RET_HEADS = 4
RET_QK = 64
RET_VD = 128
RET_CHUNK = 128
N_EXPERTS = 32
TOP_K = 4
D_FF = 2048
SWIGLU_LIMIT = 7.0
SWIGLU_ALPHA = 1.702

ROPE_BASE = 10000.0
EPS = 1e-6
Q_BLOCK = 128
N_BRANCH = 3
DIFF_W = DIFF_HEADS * DIFF_VD
MLA_W = MLA_HEADS * MLA_VD
RET_W = RET_HEADS * RET_VD
MLA_QK = MLA_NOPE + MLA_ROPE
IN_SIZES = (DIFF_HEADS * 2 * DIFF_HD, DIFF_HEADS * 2 * DIFF_HD, DIFF_W,
            MLA_Q_RANK, MLA_KV_RANK, MLA_ROPE,
            RET_HEADS * RET_QK, RET_HEADS * RET_QK, RET_W, RET_W, RET_W,
            N_BRANCH * D_MODEL)
IN_TOTAL = sum(IN_SIZES)

kernel_name = 'hybrid_diffattn_mla_retention_moe_flow_step'


def _rmsnorm(x, gain=None):
    xf = x.astype(jnp.float32)
    y = xf * lax.rsqrt(jnp.mean(xf * xf, axis=-1, keepdims=True) + EPS)
    if gain is not None:
        y = y * gain.astype(jnp.float32)
    return y.astype(x.dtype)


def _rope_1d(x, pos):
    half = x.shape[-1] // 2
    freqs = ROPE_BASE ** (-jnp.arange(half, dtype=jnp.float32) / half)
    ang = pos.astype(jnp.float32)[:, None] * freqs[None, :]
    bshape = (pos.shape[0],) + (1,) * (x.ndim - 3) + (half,)
    cos = jnp.cos(ang).reshape(bshape).astype(x.dtype)
    sin = jnp.sin(ang).reshape(bshape).astype(x.dtype)
    x1, x2 = x[..., :half], x[..., half:]
    return jnp.concatenate([x1 * cos - x2 * sin, x1 * sin + x2 * cos], axis=-1)


def _rope_axial(x, rows, cols):
    half = x.shape[-1] // 2
    return jnp.concatenate([_rope_1d(x[..., :half], rows), _rope_1d(x[..., half:], cols)], axis=-1)


def _map_query_blocks(fn, *qs):
    b, s = qs[0].shape[:2]
    nb = s // Q_BLOCK
    blocks = tuple(jnp.moveaxis(q.reshape((b, nb, Q_BLOCK) + q.shape[2:]), 1, 0) for q in qs)
    out = lax.map(lambda args: fn(*args), blocks)
    return jnp.moveaxis(out, 0, 1).reshape((b, s) + out.shape[3:])


def _softmax_attend(q, k, v, scale):
    def block(qb):
        s = jnp.einsum('bqhd,bkhd->bhqk', qb, k, preferred_element_type=jnp.float32) * scale
        p = jax.nn.softmax(s, axis=-1).astype(v.dtype)
        return jnp.einsum('bhqk,bkhe->bqhe', p, v)
    return _map_query_blocks(block, q)


def _diff_attend(q1, q2, k1, k2, v, lam):
    scale = DIFF_HD ** -0.5
    def block(q1b, q2b):
        s1 = jnp.einsum('bqhd,bkhd->bhqk', q1b, k1, preferred_element_type=jnp.float32) * scale
        s2 = jnp.einsum('bqhd,bkhd->bhqk', q2b, k2, preferred_element_type=jnp.float32) * scale
        p = jax.nn.softmax(s1, axis=-1) - lam * jax.nn.softmax(s2, axis=-1)
        return jnp.einsum('bhqk,bkhe->bqhe', p.astype(v.dtype), v)
    return _map_query_blocks(block, q1, q2)


def _mla_keys(ckv, krope, lp, grid):
    kv = jnp.einsum('bsr,rhd->bshd', ckv, lp['w_ukv'])
    k_nope, v = kv[..., :MLA_NOPE], kv[..., MLA_NOPE:]
    k_r = jnp.broadcast_to(krope[:, :, None, :], k_nope.shape[:-1] + (MLA_ROPE,))
    k = _rmsnorm(jnp.concatenate([k_nope, k_r], axis=-1), lp['mla_kn_gain'])
    if grid is not None:
        k = jnp.concatenate([k[..., :MLA_NOPE], _rope_axial(k[..., MLA_NOPE:], *grid)], axis=-1)
    return k, v


def _retention_chunkwise(q, k, v, log_g, s0):
    b, s, h, dk = q.shape
    dv = v.shape[-1]
    nc = s // RET_CHUNK
    f32 = jnp.float32
    qc = q.astype(f32).reshape(b, nc, RET_CHUNK, h, dk)
    kc = k.astype(f32).reshape(b, nc, RET_CHUNK, h, dk)
    vc = v.astype(f32).reshape(b, nc, RET_CHUNK, h, dv)
    idx = jnp.arange(RET_CHUNK, dtype=f32)
    rel = idx[:, None] - idx[None, :]
    decay_in = jnp.where(rel >= 0, jnp.exp(jnp.maximum(rel, 0.0)[None] * log_g[:, None, None]), 0.0)
    scores = jnp.einsum('bnqhd,bnchd->bnhqc', qc, kc) * decay_in
    o_inner = jnp.einsum('bnhqc,bnche->bnqhe', scores, vc)
    q_decay = jnp.exp((idx + 1.0)[None, :] * log_g[:, None])
    k_decay = jnp.exp((RET_CHUNK - 1.0 - idx)[None, :] * log_g[:, None])
    chunk_kv = jnp.einsum('bnchd,hc,bnche->nbhde', kc, k_decay, vc)
    chunk_decay = jnp.exp(RET_CHUNK * log_g)[None, :, None, None]

    def step(state, kv):
        return state * chunk_decay + kv, state

    s_final, s_before = lax.scan(step, s0.astype(f32), chunk_kv)
    o_cross = jnp.einsum('bnqhd,hq,nbhde->bnqhe', qc, q_decay, s_before)
    o = (o_inner + o_cross).reshape(b, s, h, dv)
    return o.astype(v.dtype), s_final.astype(s0.dtype)


def _mixer(h, lp, layer, ctx, pos):
    b, s, _ = h.shape
    offs = np.cumsum(IN_SIZES)[:-1].tolist()
    proj = jnp.einsum('bsd,de->bse', h, lp['w_in'])
    (dq, dk, dv, mq, mkv, mkr, rq, rk, rv, rgf, rgb, gl) = jnp.split(proj, offs, axis=-1)
    latent = pos is not None
    if latent:
        rows, cols, t = pos

    dq = _rmsnorm(dq.reshape(b, s, DIFF_HEADS, 2, DIFF_HD), lp['diff_q_gain'])
    dk = _rmsnorm(dk.reshape(b, s, DIFF_HEADS, 2, DIFF_HD), lp['diff_k_gain'])
    dv = dv.reshape(b, s, DIFF_HEADS, DIFF_VD)
    if latent:
        dq_r = _rope_axial(dq, rows, cols)
        ka = jnp.concatenate([_rope_axial(dk, rows, cols),
                              ctx['diff_k'].reshape(b, -1, DIFF_HEADS, 2, DIFF_HD)], axis=1)
        va = jnp.concatenate([dv, ctx['diff_v']], axis=1)
    else:
        dq_r, ka, va = dq, dk, dv
    lam_vec = lp['diff_lambda'].astype(jnp.float32)
    lam_init = 0.8 - 0.6 * math.exp(-0.3 * layer)
    lam = jnp.exp(jnp.sum(lam_vec[0] * lam_vec[1])) - jnp.exp(jnp.sum(lam_vec[2] * lam_vec[3])) + lam_init
    o_diff = _diff_attend(dq_r[..., 0, :], dq_r[..., 1, :], ka[..., 0, :], ka[..., 1, :], va, lam)
    y_diff = (_rmsnorm(o_diff, lp['diff_subln']) * (1.0 - lam_init)).reshape(b, s, DIFF_W)

    cq = _rmsnorm(mq, lp['mla_q_gain'])
    qm = _rmsnorm(jnp.einsum('bsr,rhd->bshd', cq, lp['w_uq']), lp['mla_qn_gain'])
    ckv = _rmsnorm(mkv, lp['mla_kv_gain'])
    if latent:
        qm = jnp.concatenate([qm[..., :MLA_NOPE], _rope_axial(qm[..., MLA_NOPE:], rows, cols)], axis=-1)
        k_lat, v_lat = _mla_keys(ckv, mkr, lp, (rows, cols))
        k_ctx, v_ctx = _mla_keys(ctx['mla_ckv'], ctx['mla_krope'], lp, None)
        km = jnp.concatenate([k_lat, k_ctx], axis=1)
        vm = jnp.concatenate([v_lat, v_ctx], axis=1)
    else:
        km, vm = _mla_keys(ckv, mkr, lp, None)
    y_mla = _softmax_attend(qm, km, vm, MLA_QK ** -0.5).reshape(b, s, MLA_W)

    rq = rq.reshape(b, s, RET_HEADS, RET_QK)
    rk = rk.reshape(b, s, RET_HEADS, RET_QK) * (RET_QK ** -0.5)
    rv = rv.reshape(b, s, RET_HEADS, RET_VD)
    if latent:
        rq = _rope_1d(rq, t)
        rk = _rope_1d(rk, t)
        s0f, s0b = ctx['ret_f'], ctx['ret_b']
    else:
        s0f = jnp.zeros((b, RET_HEADS, RET_QK, RET_VD), h.dtype)
        s0b = jnp.zeros((b, RET_HEADS, RET_QK, RET_VD), h.dtype)
    log_g = jax.nn.log_sigmoid(lp['ret_decay'].astype(jnp.float32))
    o_f, s_f = _retention_chunkwise(rq, rk, rv, log_g[0], s0f)
    o_b, s_b = _retention_chunkwise(rq[:, ::-1], rk[:, ::-1], rv[:, ::-1], log_g[1], s0b)
    o_b = o_b[:, ::-1]
    y_ret = (_rmsnorm(o_f) * jax.nn.silu(rgf.reshape(b, s, RET_HEADS, RET_VD))
             + _rmsnorm(o_b) * jax.nn.silu(rgb.reshape(b, s, RET_HEADS, RET_VD))).reshape(b, s, RET_W)

    g = jax.nn.sigmoid(gl.reshape(b, s, N_BRANCH, D_MODEL))
    merged = (g[:, :, 0] * jnp.einsum('bsw,wd->bsd', y_diff, lp['w_br_diff'])
              + g[:, :, 1] * jnp.einsum('bsw,wd->bsd', y_mla, lp['w_br_mla'])
              + g[:, :, 2] * jnp.einsum('bsw,wd->bsd', y_ret, lp['w_br_ret']))
    out = jnp.einsum('bsd,de->bse', merged, lp['w_o'])
    if latent:
        return out, None
    return out, (dk.reshape(b, s, DIFF_HEADS, 2 * DIFF_HD), dv, ckv, mkr, s_f, s_b)


def _moe(h, lp):
    b, s, d = h.shape
    x = h.reshape(b * s, d)
    logits = (x @ lp['w_router'] + lp['b_router']).astype(jnp.float32)
    top_val, top_idx = lax.top_k(logits, TOP_K)
    weights = jax.nn.softmax(top_val, axis=-1)
    gate = jnp.einsum('tk,tke->te', weights,
                      jax.nn.one_hot(top_idx, N_EXPERTS, dtype=jnp.float32)).astype(x.dtype)
    y = jnp.zeros_like(x)
    for e in range(N_EXPERTS):
        gt = jnp.minimum(x @ lp['w_gate'][e] + lp['b_gate'][e], SWIGLU_LIMIT)
        up = jnp.clip(x @ lp['w_up'][e] + lp['b_up'][e], -SWIGLU_LIMIT, SWIGLU_LIMIT)
        act = (up + 1.0) * gt * jax.nn.sigmoid(SWIGLU_ALPHA * gt)
        y = y + gate[:, e:e + 1] * (act @ lp['w_down'][e] + lp['b_down'][e])
    return y.reshape(b, s, d)


def _layer(x, cond, lp, layer, ctx, pos):
    mod = jnp.einsum('bd,de->be', jax.nn.silu(cond), lp['w_mod']) + lp['b_mod']
    sh1, sc1, g1, sh2, sc2, g2 = jnp.split(mod[:, None, :], 6, axis=-1)
    h = _rmsnorm(x, lp['norm1']) * (1.0 + sc1) + sh1
    out, new_ctx = _mixer(h, lp, layer, ctx, pos)
    x = x + g1 * out
    h = _rmsnorm(x, lp['norm2']) * (1.0 + sc2) + sh2
    x = x + g2 * _moe(h, lp)
    return x, new_ctx


def setup_inputs(seed: int = 0) -> dict:
    key = jax.random.key(seed)
    ks = jax.random.split(key, 38)

    def nrm(i, shape, scale=1.0):
        return jax.random.normal(ks[i], shape, jnp.float32) * scale

    def gain(i, shape):
        return 1.0 + nrm(i, shape, 0.02)

    g0 = 1.0 - 2.0 ** (-5.0 - np.arange(RET_HEADS, dtype=np.float32))
    decay_logit = np.log(g0 / (1.0 - g0)).astype(np.float32)
    return {
        'x_prompt': nrm(0, (BATCH, SEQ, D_MODEL)),
        'x_sample': nrm(1, (DEC_BATCH, DEC_SEQ, D_MODEL)),
        'cache_diff_k': nrm(2, (DEC_BATCH, DEPTH, PAST_LEN, DIFF_HEADS, 2 * DIFF_HD)),
        'cache_diff_v': nrm(3, (DEC_BATCH, DEPTH, PAST_LEN, DIFF_HEADS, DIFF_VD)),
        'cache_mla_ckv': nrm(4, (DEC_BATCH, DEPTH, PAST_LEN, MLA_KV_RANK)),
        'cache_mla_krope': nrm(5, (DEC_BATCH, DEPTH, PAST_LEN, MLA_ROPE)),
        'state_ret_fwd': nrm(6, (DEC_BATCH, DEPTH, RET_HEADS, RET_QK, RET_VD), 0.5),
        'state_ret_bwd': nrm(7, (DEC_BATCH, DEPTH, RET_HEADS, RET_QK, RET_VD), 0.5),
        'c': nrm(8, (DEC_BATCH, D_MODEL)),
        'c_ctx': nrm(9, (D_MODEL,)),
        'w_mod': nrm(10, (DEPTH, D_MODEL, 6 * D_MODEL), 0.5 * D_MODEL ** -0.5),
        'b_mod': nrm(11, (DEPTH, 6 * D_MODEL), 0.02),
        'norm1': gain(12, (DEPTH, D_MODEL)),
        'norm2': gain(13, (DEPTH, D_MODEL)),
        'w_in': nrm(14, (DEPTH, D_MODEL, IN_TOTAL), D_MODEL ** -0.5),
        'diff_q_gain': gain(15, (DEPTH, DIFF_HD)),
        'diff_k_gain': gain(16, (DEPTH, DIFF_HD)),
        'diff_lambda': nrm(17, (DEPTH, 4, DIFF_HD), 0.1),
        'diff_subln': gain(18, (DEPTH, DIFF_VD)),
        'mla_q_gain': gain(19, (DEPTH, MLA_Q_RANK)),
        'w_uq': nrm(20, (DEPTH, MLA_Q_RANK, MLA_HEADS, MLA_QK), MLA_Q_RANK ** -0.5),
        'mla_kv_gain': gain(21, (DEPTH, MLA_KV_RANK)),
        'w_ukv': nrm(22, (DEPTH, MLA_KV_RANK, MLA_HEADS, MLA_NOPE + MLA_VD), MLA_KV_RANK ** -0.5),
        'mla_qn_gain': gain(23, (DEPTH, MLA_QK)),
        'mla_kn_gain': gain(24, (DEPTH, MLA_QK)),
        'ret_decay': jnp.asarray(decay_logit)[None, None, :] + nrm(25, (DEPTH, 2, RET_HEADS), 0.1),
        'w_br_diff': nrm(26, (DEPTH, DIFF_W, D_MODEL), DIFF_W ** -0.5),
        'w_br_mla': nrm(27, (DEPTH, MLA_W, D_MODEL), MLA_W ** -0.5),
        'w_br_ret': nrm(28, (DEPTH, RET_W, D_MODEL), RET_W ** -0.5),
        'w_o': nrm(29, (DEPTH, D_MODEL, D_MODEL), D_MODEL ** -0.5),
        'w_router': nrm(30, (DEPTH, D_MODEL, N_EXPERTS), D_MODEL ** -0.5),
        'b_router': nrm(31, (DEPTH, N_EXPERTS), 0.01),
        'w_gate': nrm(32, (DEPTH, N_EXPERTS, D_MODEL, D_FF), D_MODEL ** -0.5),
        'b_gate': nrm(33, (DEPTH, N_EXPERTS, D_FF), 0.02),
        'w_up': nrm(34, (DEPTH, N_EXPERTS, D_MODEL, D_FF), D_MODEL ** -0.5),
        'b_up': nrm(35, (DEPTH, N_EXPERTS, D_FF), 0.02),
        'w_down': nrm(36, (DEPTH, N_EXPERTS, D_FF, D_MODEL), D_FF ** -0.5),
        'b_down': nrm(37, (DEPTH, N_EXPERTS, D_MODEL), 0.02),
    }


def reference(x_prompt, x_sample, cache_diff_k, cache_diff_v, cache_mla_ckv, cache_mla_krope,
              state_ret_fwd, state_ret_bwd, c, c_ctx, w_mod, b_mod, norm1, norm2, w_in,
              diff_q_gain, diff_k_gain, diff_lambda, diff_subln, mla_q_gain, w_uq, mla_kv_gain,
              w_ukv, mla_qn_gain, mla_kn_gain, ret_decay, w_br_diff, w_br_mla, w_br_ret, w_o,
              w_router, b_router, w_gate, b_gate, w_up, b_up, w_down, b_down):
    s_lat = x_sample.shape[1]
    ROWS = s_lat // GRID_W
    rows = jnp.repeat(jnp.arange(ROWS, dtype=jnp.int32), GRID_W)
    cols = jnp.tile(jnp.arange(GRID_W, dtype=jnp.int32), ROWS)
    t = jnp.arange(s_lat, dtype=jnp.int32)
    pos = (rows, cols, t)
    cond_ctx = c_ctx[None, :]

    xp, xs = x_prompt, x_sample
    dk_l, dv_l, ckv_l, kr_l, sf_l, sb_l = [], [], [], [], [], []
    for l in range(DEPTH):
        lp = {
            'w_mod': w_mod[l], 'b_mod': b_mod[l], 'norm1': norm1[l], 'norm2': norm2[l],
            'w_in': w_in[l], 'diff_q_gain': diff_q_gain[l], 'diff_k_gain': diff_k_gain[l],
            'diff_lambda': diff_lambda[l], 'diff_subln': diff_subln[l],
            'mla_q_gain': mla_q_gain[l], 'w_uq': w_uq[l], 'mla_kv_gain': mla_kv_gain[l],
            'w_ukv': w_ukv[l], 'mla_qn_gain': mla_qn_gain[l], 'mla_kn_gain': mla_kn_gain[l],
            'ret_decay': ret_decay[l], 'w_br_diff': w_br_diff[l], 'w_br_mla': w_br_mla[l],
            'w_br_ret': w_br_ret[l], 'w_o': w_o[l], 'w_router': w_router[l], 'b_router': b_router[l],
            'w_gate': w_gate[l], 'b_gate': b_gate[l], 'w_up': w_up[l], 'b_up': b_up[l],
            'w_down': w_down[l], 'b_down': b_down[l],
        }
        xp, st = _layer(xp, cond_ctx, lp, l, None, None)
        dk_l.append(st[0]); dv_l.append(st[1]); ckv_l.append(st[2])
        kr_l.append(st[3]); sf_l.append(st[4]); sb_l.append(st[5])
        ctx = {'diff_k': cache_diff_k[:, l], 'diff_v': cache_diff_v[:, l],
               'mla_ckv': cache_mla_ckv[:, l], 'mla_krope': cache_mla_krope[:, l],
               'ret_f': state_ret_fwd[:, l], 'ret_b': state_ret_bwd[:, l]}
        xs, _ = _layer(xs, c, lp, l, ctx, pos)

    new_diff_k = jnp.stack(dk_l, axis=1)
    new_diff_v = jnp.stack(dv_l, axis=1)
    new_mla_ckv = jnp.stack(ckv_l, axis=1)
    new_mla_krope = jnp.stack(kr_l, axis=1)
    new_ret_fwd = jnp.stack(sf_l, axis=1)
    new_ret_bwd = jnp.stack(sb_l, axis=1)
    return (xp, xs, new_diff_k, new_diff_v, new_mla_ckv, new_mla_krope, new_ret_fwd, new_ret_bwd)
```

```python
import functools
import math

import jax
import jax.numpy as jnp
import numpy as np
from jax import lax
from jax.experimental import pallas as pl
from jax.experimental.pallas import tpu as pltpu

F32 = jnp.float32
BF16 = jnp.bfloat16

GRID_W = 64
DIFF_HEADS = 4
DIFF_HD = 64
DIFF_VD = 128
MLA_HEADS = 8
MLA_Q_RANK = 512
MLA_KV_RANK = 256
MLA_NOPE = 128
MLA_ROPE = 64
MLA_VD = 128
MLA_QK = MLA_NOPE + MLA_ROPE
RET_HEADS = 4
RET_QK = 64
RET_VD = 128
RET_CHUNK = 128
TOP_K = 4
SWIGLU_LIMIT = 7.0
SWIGLU_ALPHA = 1.702
ROPE_BASE = 10000.0
EPS = 1e-6

LANES = 128
ROW_ALIGN = 256
EXPERT_CHUNK = 256
GROUP_CHUNKS = 4
COND_ROWS = 8
VMEM_LIMIT = 56 * 1024 * 1024

_SEC = dict(DQ=0, DK=512, DV=1024, MQ=1536, MKV=2048, RQ=2304, RK=2560, MKR=2816,
            RV=3072, RGF=3584, RGB=4096)
_SEC_TOTAL = 4608


def _cp(sem, vmem=VMEM_LIMIT):
    return pltpu.CompilerParams(dimension_semantics=sem, vmem_limit_bytes=vmem)


def _lane_iota(shape):
    return lax.broadcasted_iota(jnp.int32, shape, len(shape) - 1)


def _rms(x, n=None):
    n = x.shape[-1] if n is None else n
    return x * lax.rsqrt(jnp.sum(x * x, axis=-1, keepdims=True) / n + EPS)


def _dot(a, b):
    return jnp.dot(a, b, preferred_element_type=F32)


def _dot_nt(a, b):
    return lax.dot_general(a, b, (((1,), (1,)), ((), ())), preferred_element_type=F32)


def _softmax(s):
    e = jnp.exp(s - jnp.max(s, axis=-1, keepdims=True))
    return e / jnp.sum(e, axis=-1, keepdims=True)


def _sigmoid(x):
    return 1.0 / (1.0 + jnp.exp(-x))


def _rope(x, cos, sin, half):
    lane = _lane_iota(x.shape)
    up = pltpu.roll(x, LANES - half, 1)
    dn = pltpu.roll(x, half, 1)
    sw = jnp.where(lane % (2 * half) < half, up, dn)
    return x * cos + sw * sin


def _norm_halves(x, gain):
    lane = _lane_iota(x.shape)
    lo = lane < DIFF_HD
    x2 = x * x
    ss_lo = jnp.sum(jnp.where(lo, x2, 0.0), axis=-1, keepdims=True)
    ss_hi = jnp.sum(jnp.where(lo, 0.0, x2), axis=-1, keepdims=True)
    inv = jnp.where(lo, lax.rsqrt(ss_lo / DIFF_HD + EPS), lax.rsqrt(ss_hi / DIFF_HD + EPS))
    return x * inv * gain


def _mod_kernel(c_ref, w_ref, b_ref, o_ref):
    c = c_ref[...]
    a = (c * _sigmoid(c)).astype(BF16)
    o_ref[...] = _dot(a, w_ref[...].astype(BF16)) + b_ref[...]


def _mod_call(cond, w_mod, b_mod):
    L, D, N = w_mod.shape
    tn = 1024 if N % 1024 == 0 else 512
    return pl.pallas_call(
        _mod_kernel,
        grid=(L, N // tn),
        in_specs=[
            pl.BlockSpec((COND_ROWS, D), lambda l, j: (0, 0)),
            pl.BlockSpec((None, D, tn), lambda l, j: (l, 0, j)),
            pl.BlockSpec((None, 1, tn), lambda l, j: (l, 0, j)),
        ],
        out_specs=pl.BlockSpec((None, COND_ROWS, tn), lambda l, j: (l, 0, j)),
        out_shape=jax.ShapeDtypeStruct((L, COND_ROWS, N), F32),
        compiler_params=_cp(("arbitrary", "arbitrary")),
        name="adaln_mod",
    )(cond, w_mod, b_mod.reshape(L, 1, N))


def _proj_kernel(x_ref, sc_ref, sh_ref, g_ref, w_ref, o_ref, h_ref):
    @pl.when(pl.program_id(1) == 0)
    def _():
        h = _rms(x_ref[...]) * g_ref[...]
        h_ref[...] = (h * (1.0 + sc_ref[...]) + sh_ref[...]).astype(BF16)

    o_ref[...] = _dot(h_ref[...], w_ref[...])


def _proj_call(l, x, modt, norm1, w_in_p, tm, tn):
    T, D = x.shape
    PC = w_in_p.shape[-1]
    return pl.pallas_call(
        _proj_kernel,
        grid=(T // tm, PC // tn),
        in_specs=[
            pl.BlockSpec((tm, D), lambda i, j: (i, 0)),
            pl.BlockSpec((None, 1, D), lambda i, j: (i, 0, 1)),
            pl.BlockSpec((None, 1, D), lambda i, j: (i, 0, 0)),
            pl.BlockSpec((None, 1, D), lambda i, j: (l, 0, 0)),
            pl.BlockSpec((None, D, tn), lambda i, j: (l, 0, j)),
        ],
        out_specs=pl.BlockSpec((tm, tn), lambda i, j: (i, j)),
        out_shape=jax.ShapeDtypeStruct((T, PC), F32),
        scratch_shapes=[pltpu.VMEM((tm, D), BF16)],
        compiler_params=_cp(("arbitrary", "arbitrary")),
        name="norm_proj",
    )(x, modt, modt, norm1, w_in_p)


def _diff_kernel(*refs, latent, S, P, lam_init, l):
    if latent:
        (lam_ref, q_ref, k_ref, v_ref, qg_ref, kg_ref, sg_ref, ck_ref, cv_ref,
         cosq_ref, sinq_ref, cosk_ref, sink_ref, y_ref, kall, vall) = refs
    else:
        (lam_ref, q_ref, k_ref, v_ref, qg_ref, kg_ref, sg_ref, y_ref, kn_ref, kall, vall) = refs

    @pl.when(pl.program_id(1) == 0)
    def _():
        for h in range(DIFF_HEADS):
            sl = slice(LANES * h, LANES * (h + 1))
            kn = _norm_halves(k_ref[:, sl], kg_ref[...])
            if latent:
                kn = _rope(kn, cosk_ref[...], sink_ref[...], 16)
            else:
                kn_ref[:, sl] = kn
            kall[0:S, sl] = kn.astype(BF16)
        vall[0:S, :] = v_ref[...].astype(BF16)
        if latent:
            kall[S:S + P, :] = ck_ref[...].astype(BF16)
            vall[S:S + P, :] = cv_ref[...].astype(BF16)

    lam = lam_ref[l]
    scale = DIFF_HD ** -0.5
    for h in range(DIFF_HEADS):
        sl = slice(LANES * h, LANES * (h + 1))
        qn = _norm_halves(q_ref[:, sl], qg_ref[...])
        if latent:
            qn = _rope(qn, cosq_ref[...], sinq_ref[...], 16)
        lo = _lane_iota(qn.shape) < DIFF_HD
        q1 = jnp.where(lo, qn, 0.0).astype(BF16)
        q2 = jnp.where(lo, 0.0, qn).astype(BF16)
        kh = kall[:, sl]
        p = _softmax(_dot_nt(q1, kh) * scale) - lam * _softmax(_dot_nt(q2, kh) * scale)
        o = _dot(p.astype(BF16), vall[:, sl])
        y_ref[:, sl] = (_rms(o) * sg_ref[...] * (1.0 - lam_init)).astype(BF16)


def _diff_call(l, proj, off, lam, gains, B, S, row0, tq, latent, cache=None, tabs=None):
    qg, kg, sg = gains
    W = DIFF_HEADS * LANES
    nq = S // tq
    P = cache[0].shape[2] if latent else 0
    lam_init = 0.8 - 0.6 * math.exp(-0.3 * l)
    cq, ck, cv = ((off + _SEC[n]) // W for n in ("DQ", "DK", "DV"))
    in_specs = [
        pl.BlockSpec(memory_space=pltpu.SMEM),
        pl.BlockSpec((tq, W), lambda b, i: (row0 // tq + b * nq + i, cq)),
        pl.BlockSpec((S, W), lambda b, i: (row0 // S + b, ck)),
        pl.BlockSpec((S, W), lambda b, i: (row0 // S + b, cv)),
        pl.BlockSpec((None, 1, LANES), lambda b, i: (l, 0, 0)),
        pl.BlockSpec((None, 1, LANES), lambda b, i: (l, 0, 0)),
        pl.BlockSpec((None, 1, LANES), lambda b, i: (l, 0, 0)),
    ]
    args = [lam, proj, proj, proj, qg, kg, sg]
    out_shape = [jax.ShapeDtypeStruct((B * S, W), BF16)]
    out_specs = [pl.BlockSpec((tq, W), lambda b, i: (b * nq + i, 0))]
    if latent:
        cos, sin = tabs
        in_specs += [
            pl.BlockSpec((None, None, P, W), lambda b, i: (b, l, 0, 0)),
            pl.BlockSpec((None, None, P, W), lambda b, i: (b, l, 0, 0)),
            pl.BlockSpec((tq, LANES), lambda b, i: (i, 0)),
            pl.BlockSpec((tq, LANES), lambda b, i: (i, 0)),
            pl.BlockSpec((S, LANES), lambda b, i: (0, 0)),
            pl.BlockSpec((S, LANES), lambda b, i: (0, 0)),
        ]
        args += [cache[0], cache[1], cos, sin, cos, sin]
    else:
        out_shape.append(jax.ShapeDtypeStruct((B * S, W), F32))
        out_specs.append(pl.BlockSpec((S, W), lambda b, i: (b, 0)))
    return pl.pallas_call(
        functools.partial(_diff_kernel, latent=latent, S=S, P=P, lam_init=lam_init, l=l),
        grid=(B, nq),
        in_specs=in_specs,
        out_specs=out_specs,
        out_shape=out_shape,
        scratch_shapes=[pltpu.VMEM((S + P, W), BF16), pltpu.VMEM((S + P, W), BF16)],
        compiler_params=_cp(("arbitrary", "arbitrary")),
        name="diff_attn_lat" if latent else "diff_attn_ctx",
    )(*args)


def _mla_kernel(*refs, latent, S, P):
    if latent:
        (mq_ref, mkv_ref, mkr_ref, wuq_ref, wukv_ref, qg_ref, kvg_ref, qng_ref, kng_ref,
         cckv_ref, ckr_ref, cosq_ref, sinq_ref, cosk_ref, sink_ref, y_ref, kall, vall) = refs
    else:
        (mq_ref, mkv_ref, mkr_ref, wuq_ref, wukv_ref, qg_ref, kvg_ref, qng_ref, kng_ref,
         y_ref, ckv_ref, kall, vall) = refs
    HW = MLA_NOPE + MLA_VD

    @pl.when(pl.program_id(1) == 0)
    def _():
        ckv = _rms(mkv_ref[...]) * kvg_ref[...]
        if not latent:
            ckv_ref[...] = ckv
        segs = [(0, S, ckv, mkr_ref[...], latent)]
        if latent:
            segs.append((S, P, cckv_ref[...], ckr_ref[...], False))
        for r0, R, cv, kr, rot in segs:
            kv = _dot(cv.astype(BF16), wukv_ref[...])
            kr_ss = jnp.sum(kr * kr, axis=-1, keepdims=True)
            krg = kr * kng_ref[:, MLA_NOPE:]
            if rot:
                krg = _rope(krg, cosk_ref[...], sink_ref[...], 16)
            for h in range(MLA_HEADS):
                kn = kv[:, HW * h:HW * h + MLA_NOPE]
                ss = jnp.sum(kn * kn, axis=-1, keepdims=True) + kr_ss
                r = lax.rsqrt(ss / MLA_QK + EPS)
                kall[h, r0:r0 + R, 0:MLA_NOPE] = (kn * r * kng_ref[:, 0:MLA_NOPE]).astype(BF16)
                kall[h, r0:r0 + R, MLA_NOPE:] = (krg * r).astype(BF16)
                vall[h, r0:r0 + R, :] = kv[:, HW * h + MLA_NOPE:HW * (h + 1)].astype(BF16)

    cq = (_rms(mq_ref[...]) * qg_ref[...]).astype(BF16)
    qm = _dot(cq, wuq_ref[...])
    scale = MLA_QK ** -0.5
    for h in range(MLA_HEADS):
        qn = _rms(qm[:, HW * h:HW * (h + 1)], MLA_QK) * qng_ref[...]
        if latent:
            qr = _rope(qn[:, MLA_NOPE:], cosq_ref[...], sinq_ref[...], 16)
            qn = jnp.concatenate([qn[:, 0:MLA_NOPE], qr], axis=-1)
        p = _softmax(_dot_nt(qn.astype(BF16), kall[h]) * scale)
        y_ref[:, MLA_VD * h:MLA_VD * (h + 1)] = _dot(p.astype(BF16), vall[h]).astype(BF16)


def _mla_call(l, proj, off, weights, B, S, row0, tq, latent, cache=None, tabs=None):
    wuq, wukv, qg, kvg, qng, kng = weights
    nq = S // tq
    P = cache[0].shape[2] if latent else 0
    HW = MLA_NOPE + MLA_VD
    NW = MLA_HEADS * HW
    c_mq = (off + _SEC["MQ"]) // MLA_Q_RANK
    c_mkv = (off + _SEC["MKV"]) // MLA_KV_RANK
    c_mkr = (off + _SEC["MKR"]) // LANES

    def lay(shape):
        return pl.BlockSpec((None,) + shape, lambda b, i: (l,) + (0,) * len(shape))

    in_specs = [
        pl.BlockSpec((tq, MLA_Q_RANK), lambda b, i: (row0 // tq + b * nq + i, c_mq)),
        pl.BlockSpec((S, MLA_KV_RANK), lambda b, i: (row0 // S + b, c_mkv)),
        pl.BlockSpec((S, LANES), lambda b, i: (row0 // S + b, c_mkr)),
        lay((MLA_Q_RANK, NW)), lay((MLA_KV_RANK, NW)),
        lay((1, MLA_Q_RANK)), lay((1, MLA_KV_RANK)), lay((1, HW)), lay((1, HW)),
    ]
    args = [proj, proj, proj, wuq, wukv, qg, kvg, qng, kng]
    out_shape = [jax.ShapeDtypeStruct((B * S, MLA_HEADS * MLA_VD), BF16)]
    out_specs = [pl.BlockSpec((tq, MLA_HEADS * MLA_VD), lambda b, i: (b * nq + i, 0))]
    if latent:
        cos, sin = tabs
        in_specs += [
            pl.BlockSpec((None, None, P, MLA_KV_RANK), lambda b, i: (b, l, 0, 0)),
            pl.BlockSpec((None, None, P, LANES), lambda b, i: (b, l, 0, 0)),
            pl.BlockSpec((tq, LANES), lambda b, i: (i, 0)),
            pl.BlockSpec((tq, LANES), lambda b, i: (i, 0)),
            pl.BlockSpec((S, LANES), lambda b, i: (0, 0)),
            pl.BlockSpec((S, LANES), lambda b, i: (0, 0)),
        ]
        args += [cache[0], cache[1], cos, sin, cos, sin]
    else:
        out_shape.append(jax.ShapeDtypeStruct((B * S, MLA_KV_RANK), F32))
        out_specs.append(pl.BlockSpec((S, MLA_KV_RANK), lambda b, i: (b, 0)))
    return pl.pallas_call(
        functools.partial(_mla_kernel, latent=latent, S=S, P=P),
        grid=(B, nq),
        in_specs=in_specs,
        out_specs=out_specs,
        out_shape=out_shape,
        scratch_shapes=[pltpu.VMEM((MLA_HEADS, S + P, HW), BF16),
                        pltpu.VMEM((MLA_HEADS, S + P, MLA_VD), BF16)],
        compiler_params=_cp(("arbitrary", "arbitrary")),
        name="mla_lat" if latent else "mla_ctx",
    )(*args)


def _ret_kernel(*refs, latent, S, l):
    if latent:
        (lg_ref, rq_ref, rk_ref, rv_ref, gf_ref, gb_ref, cos_ref, sin_ref, s0f_ref, s0b_ref,
         y_ref, qs, ks, of, ob) = refs
    else:
        (lg_ref, rq_ref, rk_ref, rv_ref, gf_ref, gb_ref, y_ref, sf_ref, sb_ref, qs, ks, of, ob) = refs
    C = RET_CHUNK
    nc = S // C
    ii = lax.broadcasted_iota(jnp.int32, (C, C), 0).astype(F32)
    jj = lax.broadcasted_iota(jnp.int32, (C, C), 1).astype(F32)
    col = lax.broadcasted_iota(jnp.int32, (C, 1), 0).astype(F32)
    one = jnp.ones((1, 1), F32)
    for s in range(RET_HEADS // 2):
        ssl = slice(LANES * s, LANES * (s + 1))
        qslab = rq_ref[:, ssl]
        kslab = rk_ref[:, ssl] * (RET_QK ** -0.5)
        if latent:
            qslab = _rope(qslab, cos_ref[...], sin_ref[...], 32)
            kslab = _rope(kslab, cos_ref[...], sin_ref[...], 32)
        lo = _lane_iota(qslab.shape) < RET_QK
        for u in range(2):
            h = 2 * s + u
            vsl = slice(RET_VD * h, RET_VD * (h + 1))
            keep = lo if u == 0 else jnp.logical_not(lo)
            qs[...] = jnp.where(keep, qslab, 0.0)
            ks[...] = jnp.where(keep, kslab, 0.0)
            for d in range(2):
                lg = lg_ref[(l * 2 + d) * RET_HEADS + h]
                if d == 0:
                    rel = ii - jj
                    qdec = jnp.exp((col + 1.0) * lg)
                    kdec = jnp.exp((C - 1.0 - col) * lg)
                else:
                    rel = jj - ii
                    qdec = jnp.exp((C - col) * lg)
                    kdec = jnp.exp(col * lg)
                dmat = jnp.where(rel >= 0, jnp.exp(jnp.maximum(rel, 0.0) * lg), 0.0)
                cdec = jnp.exp(one * (C * lg))
                o_ref = of if d == 0 else ob
                if latent:
                    s0 = (s0f_ref if d == 0 else s0b_ref)[h]
                    z = jnp.zeros_like(s0)
                    st0 = jnp.concatenate([s0, z] if u == 0 else [z, s0], axis=0)
                else:
                    st0 = jnp.zeros((LANES, RET_VD), F32)

                def body(n, st, d=d, dmat=dmat, qdec=qdec, kdec=kdec, cdec=cdec, o_ref=o_ref, vsl=vsl):
                    cidx = n if d == 0 else nc - 1 - n
                    r0 = pl.multiple_of(cidx * C, C)
                    qc = qs[pl.ds(r0, C), :]
                    kc = ks[pl.ds(r0, C), :]
                    vc = rv_ref[pl.ds(r0, C), vsl].astype(BF16)
                    sc = _dot_nt(qc.astype(BF16), kc.astype(BF16)) * dmat
                    o = _dot(sc.astype(BF16), vc) + _dot((qc * qdec).astype(BF16), st.astype(BF16))
                    o_ref[pl.ds(r0, C), :] = o
                    kt = jnp.transpose(kc * kdec).astype(BF16)
                    return st * cdec + _dot(kt, vc)

                st = lax.fori_loop(0, nc, body, st0)
                if not latent:
                    (sf_ref if d == 0 else sb_ref)[h] = st[RET_QK * u:RET_QK * (u + 1), :]
            gf = gf_ref[:, vsl]
            gb = gb_ref[:, vsl]
            y = _rms(of[...]) * (gf * _sigmoid(gf)) + _rms(ob[...]) * (gb * _sigmoid(gb))
            y_ref[:, vsl] = y.astype(BF16)


def _ret_call(l, proj, off, lg, B, S, row0, latent, states=None, tabs=None):
    QW = RET_HEADS * RET_QK
    VW = RET_HEADS * RET_VD
    c_rq, c_rk = (off + _SEC["RQ"]) // QW, (off + _SEC["RK"]) // QW
    c_rv, c_gf, c_gb = ((off + _SEC[n]) // VW for n in ("RV", "RGF", "RGB"))
    rb = row0 // S
    in_specs = [
        pl.BlockSpec(memory_space=pltpu.SMEM),
        pl.BlockSpec((S, QW), lambda b: (rb + b, c_rq)),
        pl.BlockSpec((S, QW), lambda b: (rb + b, c_rk)),
        pl.BlockSpec((S, VW), lambda b: (rb + b, c_rv)),
        pl.BlockSpec((S, VW), lambda b: (rb + b, c_gf)),
        pl.BlockSpec((S, VW), lambda b: (rb + b, c_gb)),
    ]
    args = [lg, proj, proj, proj, proj, proj]
    out_shape = [jax.ShapeDtypeStruct((B * S, VW), BF16)]
    out_specs = [pl.BlockSpec((S, VW), lambda b: (b, 0))]
    st_block = (None, RET_HEADS, RET_QK, RET_VD)
    if latent:
        cos, sin = tabs
        in_specs += [
            pl.BlockSpec((S, LANES), lambda b: (0, 0)),
            pl.BlockSpec((S, LANES), lambda b: (0, 0)),
            pl.BlockSpec((None,) + st_block, lambda b: (b, l, 0, 0, 0)),
            pl.BlockSpec((None,) + st_block, lambda b: (b, l, 0, 0, 0)),
        ]
        args += [cos, sin, states[0], states[1]]
    else:
        for _ in range(2):
            out_shape.append(jax.ShapeDtypeStruct((B, RET_HEADS, RET_QK, RET_VD), F32))
            out_specs.append(pl.BlockSpec(st_block, lambda b: (b, 0, 0, 0)))
    return pl.pallas_call(
        functools.partial(_ret_kernel, latent=latent, S=S, l=l),
        grid=(B,),
        in_specs=in_specs,
        out_specs=out_specs,
        out_shape=out_shape,
        scratch_shapes=[pltpu.VMEM((S, LANES), F32), pltpu.VMEM((S, LANES), F32),
                        pltpu.VMEM((S, RET_VD), F32), pltpu.VMEM((S, RET_VD), F32)],
        compiler_params=_cp(("arbitrary",)),
        name="retention_lat" if latent else "retention_ctx",
    )(*args)


def _merge_kernel(yd_ref, ym_ref, yr_ref, g0_ref, g1_ref, g2_ref, wd_ref, wm_ref, wr_ref, o_ref):
    acc = _sigmoid(g0_ref[...]) * _dot(yd_ref[...], wd_ref[...])
    acc = acc + _sigmoid(g1_ref[...]) * _dot(ym_ref[...], wm_ref[...])
    acc = acc + _sigmoid(g2_ref[...]) * _dot(yr_ref[...], wr_ref[...])
    o_ref[...] = acc.astype(BF16)


def _merge_call(l, yd, ym, yr, proj, wd, wm, wr, tm):
    T = yd.shape[0]
    D = wd.shape[-1]

    def rows(w):
        return pl.BlockSpec((tm, w), lambda i: (i, 0))

    def gate(k):
        return pl.BlockSpec((tm, D), lambda i: (i, k))

    def wt(w):
        return pl.BlockSpec((None, w.shape[1], D), lambda i: (l, 0, 0))

    return pl.pallas_call(
        _merge_kernel,
        grid=(T // tm,),
        in_specs=[rows(yd.shape[1]), rows(ym.shape[1]), rows(yr.shape[1]),
                  gate(0), gate(1), gate(2), wt(wd), wt(wm), wt(wr)],
        out_specs=pl.BlockSpec((tm, D), lambda i: (i, 0)),
        out_shape=jax.ShapeDtypeStruct((T, D), BF16),
        compiler_params=_cp(("arbitrary",)),
        name="branch_merge",
    )(yd, ym, yr, proj, proj, proj, wd, wm, wr)


def _post_kernel(m_ref, x_ref, wo_ref, g1_ref, sc_ref, sh_ref, n2_ref, wr_ref, br_ref,
                 x1_ref, h2_ref, route_ref, cnt_ref, run_ref, *, E, tm):
    @pl.when(pl.program_id(0) == 0)
    def _():
        run_ref[...] = jnp.zeros_like(run_ref)

    x1 = x_ref[...] + g1_ref[...] * _dot(m_ref[...], wo_ref[...])
    x1_ref[...] = x1
    h = _rms(x1) * n2_ref[...]
    h = h * (1.0 + sc_ref[...]) + sh_ref[...]
    h2_ref[...] = h
    logits = jnp.dot(h, wr_ref[...], preferred_element_type=F32,
                     precision=lax.Precision.HIGHEST) + br_ref[...]
    lane = _lane_iota(logits.shape)
    cur = jnp.where(lane < E, logits, -jnp.inf)
    hots, vals = [], []
    for _ in range(TOP_K):
        m = jnp.max(cur, axis=-1, keepdims=True)
        idx = jnp.min(jnp.where(cur == m, lane, LANES), axis=-1, keepdims=True)
        hot = lane == idx
        hots.append(hot)
        vals.append(m)
        cur = jnp.where(hot, -jnp.inf, cur)
    exps = [jnp.exp(v - vals[0]) for v in vals]
    den = exps[0] + exps[1] + exps[2] + exps[3]
    sel = jnp.zeros(logits.shape, F32)
    for hot in hots:
        sel = sel + jnp.where(hot, 1.0, 0.0)
    ri = lax.broadcasted_iota(jnp.int32, (tm, tm), 0)
    ci = lax.broadcasted_iota(jnp.int32, (tm, tm), 1)
    tri = jnp.where(ci < ri, 1.0, 0.0).astype(BF16)
    rank = _dot(tri, sel.astype(BF16)) + run_ref[0:1, :]
    lane_f = lane.astype(F32)
    route = jnp.zeros(logits.shape, F32)
    for k in range(TOP_K):
        e_k = jnp.sum(jnp.where(hots[k], lane_f, 0.0), axis=-1, keepdims=True)
        r_k = jnp.sum(jnp.where(hots[k], rank, 0.0), axis=-1, keepdims=True)
        route = jnp.where(lane == k, e_k, route)
        route = jnp.where(lane == TOP_K + k, r_k, route)
        route = jnp.where(lane == 2 * TOP_K + k, exps[k] / den, route)
    route_ref[...] = route
    total = run_ref[...] + jnp.sum(sel, axis=0, keepdims=True)
    run_ref[...] = total
    cnt_ref[...] = total


def _post_call(l, merged, x, wo, modt, norm2, w_router, b_router, E, tm):
    T, D = x.shape

    def modc(k):
        return pl.BlockSpec((None, 1, D), lambda i: (i, 0, k))

    return pl.pallas_call(
        functools.partial(_post_kernel, E=E, tm=tm),
        grid=(T // tm,),
        in_specs=[
            pl.BlockSpec((tm, D), lambda i: (i, 0)),
            pl.BlockSpec((tm, D), lambda i: (i, 0)),
            pl.BlockSpec((None, D, D), lambda i: (l, 0, 0)),
            modc(2), modc(4), modc(3),
            pl.BlockSpec((None, 1, D), lambda i: (l, 0, 0)),
            pl.BlockSpec((None, D, LANES), lambda i: (l, 0, 0)),
            pl.BlockSpec((None, 1, LANES), lambda i: (l, 0, 0)),
        ],
        out_specs=[
            pl.BlockSpec((tm, D), lambda i: (i, 0)),
            pl.BlockSpec((tm, D), lambda i: (i, 0)),
            pl.BlockSpec((tm, LANES), lambda i: (i, 0)),
            pl.BlockSpec((COND_ROWS, LANES), lambda i: (0, 0)),
        ],
        out_shape=[
            jax.ShapeDtypeStruct((T, D), F32),
            jax.ShapeDtypeStruct((T, D), F32),
            jax.ShapeDtypeStruct((T, LANES), F32),
            jax.ShapeDtypeStruct((COND_ROWS, LANES), F32),
        ],
        scratch_shapes=[pltpu.VMEM((COND_ROWS, LANES), F32)],
        compiler_params=_cp(("arbitrary",)),
        name="out_proj_router",
    )(merged, x, wo, modt, modt, modt, norm2, w_router, b_router)


def _dispatch_kernel(dest_ref, pstart_ref, pcnt_ref, h_hbm, xs_hbm, sem, *, T, E, U):
    def row_copy(src, dst):
        return pltpu.make_async_copy(h_hbm.at[pl.ds(src, 1)], xs_hbm.at[pl.ds(dst, 1)], sem)

    def batch_wait():
        pltpu.make_async_copy(h_hbm.at[pl.ds(0, U * TOP_K)], xs_hbm.at[pl.ds(0, U * TOP_K)], sem).wait()

    def issue(b):
        for u in range(U):
            t = b * U + u
            for k in range(TOP_K):
                row_copy(t, dest_ref[t * TOP_K + k]).start()

    issue(0)

    def body(b, c):
        issue(b)
        batch_wait()
        return c

    lax.fori_loop(1, T // U, body, 0)
    batch_wait()

    def pad_body(e, c):
        s = pstart_ref[e]
        n = pcnt_ref[e]
        lax.fori_loop(0, n, lambda r, c2: (row_copy(0, s + r).start(), c2)[1], 0)
        lax.fori_loop(0, n, lambda r, c2: (row_copy(0, 0).wait(), c2)[1], 0)
        return c

    lax.fori_loop(0, E, pad_body, 0)


def _dispatch_call(h2, dest, pstart, pcnt, rows, E):
    T, D = h2.shape
    return pl.pallas_call(
        functools.partial(_dispatch_kernel, T=T, E=E, U=8),
        grid_spec=pltpu.PrefetchScalarGridSpec(
            num_scalar_prefetch=3,
            grid=(1,),
            in_specs=[pl.BlockSpec(memory_space=pl.ANY)],
            out_specs=pl.BlockSpec(memory_space=pl.ANY),
            scratch_shapes=[pltpu.SemaphoreType.DMA],
        ),
        out_shape=jax.ShapeDtypeStruct((rows, D), F32),
        compiler_params=_cp(("arbitrary",)),
        name="expert_dispatch",
    )(dest, pstart, pcnt, h2)


def _expert_kernel(ge_ref, gs_ref, gn_ref, xs_hbm, wg_ref, wu_ref, wd_ref, bg_ref, bu_ref, bd_ref,
                   os_hbm, xstage, xb, ybuf, wgb, wub, wdb, sem_in, sem_out, *, NF):
    g = pl.program_id(0)
    f = pl.program_id(1)
    n = gn_ref[g]
    start = pl.multiple_of(gs_ref[g], EXPERT_CHUNK)

    def chunk_rows(c):
        return pl.ds(pl.multiple_of(start + c * EXPERT_CHUNK, EXPERT_CHUNK), EXPERT_CHUNK)

    @pl.when(n > 0)
    def _():
        @pl.when(f == 0)
        def _():
            for c in range(GROUP_CHUNKS):
                @pl.when(c < n)
                def _():
                    cp = pltpu.make_async_copy(xs_hbm.at[chunk_rows(c)], xstage, sem_in)
                    cp.start()
                    cp.wait()
                    xb[c] = xstage[...].astype(BF16)

        wgb[...] = wg_ref[...].astype(BF16)
        wub[...] = wu_ref[...].astype(BF16)
        wdb[...] = wd_ref[...].astype(BF16)
        for c in range(GROUP_CHUNKS):
            @pl.when(c < n)
            def _():
                x = xb[c]
                gt = jnp.minimum(_dot(x, wgb[...]) + bg_ref[...], SWIGLU_LIMIT)
                up = jnp.clip(_dot(x, wub[...]) + bu_ref[...], -SWIGLU_LIMIT, SWIGLU_LIMIT)
                act = ((up + 1.0) * gt * _sigmoid(SWIGLU_ALPHA * gt)).astype(BF16)
                part = _dot(act, wdb[...])

                @pl.when(f == 0)
                def _():
                    ybuf[c] = part + bd_ref[...]

                @pl.when(f > 0)
                def _():
                    ybuf[c] = ybuf[c] + part

        @pl.when(f == NF - 1)
        def _():
            for c in range(GROUP_CHUNKS):
                @pl.when(c < n)
                def _():
                    cp = pltpu.make_async_copy(ybuf.at[c], os_hbm.at[chunk_rows(c)], sem_out)
                    cp.start()
                    cp.wait()


def _expert_call(l, xs, ge, gs, gn, w_gate, w_up, w_down, b_gate, b_up, b_down, tf):
    rows, D = xs.shape
    F = w_gate.shape[-1]
    NF = F // tf
    G = ge.shape[0]

    def fsel(f, gn, g):
        return jnp.where(gn[g] > 0, f, NF - 1)

    return pl.pallas_call(
        functools.partial(_expert_kernel, NF=NF),
        grid_spec=pltpu.PrefetchScalarGridSpec(
            num_scalar_prefetch=3,
            grid=(G, NF),
            in_specs=[
                pl.BlockSpec(memory_space=pl.ANY),
                pl.BlockSpec((None, None, D, tf), lambda g, f, ge, gs, gn: (l, ge[g], 0, fsel(f, gn, g))),
                pl.BlockSpec((None, None, D, tf), lambda g, f, ge, gs, gn: (l, ge[g], 0, fsel(f, gn, g))),
                pl.BlockSpec((None, None, tf, D), lambda g, f, ge, gs, gn: (l, ge[g], fsel(f, gn, g), 0)),
                pl.BlockSpec((None, None, 1, tf), lambda g, f, ge, gs, gn: (l, ge[g], 0, fsel(f, gn, g))),
                pl.BlockSpec((None, None, 1, tf), lambda g, f, ge, gs, gn: (l, ge[g], 0, fsel(f, gn, g))),
                pl.BlockSpec((None, None, 1, D), lambda g, f, ge, gs, gn: (l, ge[g], 0, 0)),
            ],
            out_specs=pl.BlockSpec(memory_space=pl.ANY),
            scratch_shapes=[
                pltpu.VMEM((EXPERT_CHUNK, D), F32),
                pltpu.VMEM((GROUP_CHUNKS, EXPERT_CHUNK, D), BF16),
                pltpu.VMEM((GROUP_CHUNKS, EXPERT_CHUNK, D), F32),
                pltpu.VMEM((D, tf), BF16),
                pltpu.VMEM((D, tf), BF16),
                pltpu.VMEM((tf, D), BF16),
                pltpu.SemaphoreType.DMA,
                pltpu.SemaphoreType.DMA,
            ],
        ),
        out_shape=jax.ShapeDtypeStruct((rows, D), F32),
        compiler_params=_cp(("arbitrary", "arbitrary")),
        name="grouped_experts",
    )(ge, gs, gn, xs, w_gate, w_up, w_down, b_gate, b_up, b_down)


def _combine_kernel(dest_ref, os_hbm, x1_ref, g2_ref, route_ref, o_ref, buf, sem, *, tm):
    i = pl.program_id(0)
    nt = pl.num_programs(0)

    def issue(tile, slot):
        def body(r, c):
            t = tile * tm + r
            for k in range(TOP_K):
                pltpu.make_async_copy(os_hbm.at[pl.ds(dest_ref[t * TOP_K + k], 1)],
                                      buf.at[slot, k, pl.ds(r, 1)], sem.at[slot]).start()
            return c

        lax.fori_loop(0, tm, body, 0)

    @pl.when(i == 0)
    def _():
        issue(0, 0)

    @pl.when(i + 1 < nt)
    def _():
        issue(i + 1, (i + 1) % 2)

    slot = i % 2
    for k in range(TOP_K):
        pltpu.make_async_copy(os_hbm.at[pl.ds(0, tm)], buf.at[slot, k], sem.at[slot]).wait()
    route = route_ref[...]
    y = route[:, 2 * TOP_K:2 * TOP_K + 1] * buf[slot, 0]
    for k in range(1, TOP_K):
        y = y + route[:, 2 * TOP_K + k:2 * TOP_K + k + 1] * buf[slot, k]
    o_ref[...] = x1_ref[...] + g2_ref[...] * y


def _combine_call(dest, os_, x1, modt, route, tm):
    T, D = x1.shape
    return pl.pallas_call(
        functools.partial(_combine_kernel, tm=tm),
        grid_spec=pltpu.PrefetchScalarGridSpec(
            num_scalar_prefetch=1,
            grid=(T // tm,),
            in_specs=[
                pl.BlockSpec(memory_space=pl.ANY),
                pl.BlockSpec((tm, D), lambda i, d: (i, 0)),
                pl.BlockSpec((None, 1, D), lambda i, d: (i, 0, 5)),
                pl.BlockSpec((tm, LANES), lambda i, d: (i, 0)),
            ],
            out_specs=pl.BlockSpec((tm, D), lambda i, d: (i, 0)),
            scratch_shapes=[pltpu.VMEM((2, TOP_K, tm, D), F32), pltpu.SemaphoreType.DMA((2,))],
        ),
        out_shape=jax.ShapeDtypeStruct((T, D), F32),
        compiler_params=_cp(("arbitrary",)),
        name="expert_combine",
    )(dest, os_, x1, modt, route)


def _rope_tables(pos_a, pos_b, half, reps):
    freqs = ROPE_BASE ** (-jnp.arange(half, dtype=F32) / half)

    def blk(pos):
        ang = pos.astype(F32)[:, None] * freqs[None, :]
        c, s = jnp.cos(ang), jnp.sin(ang)
        return jnp.concatenate([c, c], -1), jnp.concatenate([-s, s], -1)

    ca, sa = blk(pos_a)
    cb, sb = blk(pos_b)
    cos = jnp.tile(jnp.concatenate([ca, cb], -1), (1, reps))
    sin = jnp.tile(jnp.concatenate([sa, sb], -1), (1, reps))
    return cos, sin


def _cond_tiles(T, Tp, Ss, tm):
    starts = np.arange(0, T, tm)
    return np.where(starts < Tp, 0, 1 + np.maximum(starts - Tp, 0) // Ss).astype(np.int32)


def _pick_tile(limit, *sizes):
    t = ROW_ALIGN
    while t * 2 <= limit and all(s % (t * 2) == 0 for s in sizes):
        t *= 2
    return t


def _route_plan(route, cnt, E, G):
    idx = route[:, 0:TOP_K].astype(jnp.int32)
    rank = route[:, TOP_K:2 * TOP_K].astype(jnp.int32)
    counts = cnt[0, :E].astype(jnp.int32)
    nch = (counts + EXPERT_CHUNK - 1) // EXPERT_CHUNK
    padded = nch * EXPERT_CHUNK
    base = jnp.cumsum(padded) - padded
    dest = (base[idx] + rank).reshape(-1)
    ngr = (nch + GROUP_CHUNKS - 1) // GROUP_CHUNKS
    gend = jnp.cumsum(ngr)
    gid = jnp.arange(G, dtype=jnp.int32)
    valid = gid < gend[-1]
    ge = jnp.minimum(jnp.searchsorted(gend, gid, side="right"), E - 1).astype(jnp.int32)
    j = gid - (gend - ngr)[ge]
    gs = base[ge] + j * (GROUP_CHUNKS * EXPERT_CHUNK)
    gn = jnp.clip(nch[ge] - j * GROUP_CHUNKS, 0, GROUP_CHUNKS)
    last = jnp.maximum(gend[-1] - 1, 0)
    ge = jnp.where(valid, ge, ge[last])
    gs = jnp.where(valid, gs, 0)
    gn = jnp.where(valid, gn, 0)
    return (dest.astype(jnp.int32), (base + counts).astype(jnp.int32), (padded - counts).astype(jnp.int32),
            ge.astype(jnp.int32), gs.astype(jnp.int32), gn.astype(jnp.int32))


def kernel(x_prompt, x_sample, cache_diff_k, cache_diff_v, cache_mla_ckv, cache_mla_krope, state_ret_fwd, state_ret_bwd, c, c_ctx, w_mod, b_mod, norm1, norm2, w_in, diff_q_gain, diff_k_gain, diff_lambda, diff_subln, mla_q_gain, w_uq, mla_kv_gain, w_ukv, mla_qn_gain, mla_kn_gain, ret_decay, w_br_diff, w_br_mla, w_br_ret, w_o, w_router, b_router, w_gate, b_gate, w_up, b_up, w_down, b_down):
    Bp, Sp, D = x_prompt.shape
    Bs, Ss, _ = x_sample.shape
    L = w_mod.shape[0]
    P = cache_diff_k.shape[2]
    E = w_router.shape[-1]
    F = w_gate.shape[-1]
    Tp, Ts = Bp * Sp, Bs * Ss
    T = Tp + Ts
    assert 1 + Bs <= COND_ROWS and E <= LANES
    assert Tp % ROW_ALIGN == 0 and Ss % ROW_ALIGN == 0 and Tp % Ss == 0 and Sp % RET_CHUNK == 0
    assert (3 * D) % 512 == 0 and F % 512 == 0

    off = 3 * D
    o_gl = sum((512, 512, 512, 512, 256, 64, 256, 256, 512, 512, 512))
    w_in_p = jnp.concatenate([
        w_in[..., o_gl:], w_in[..., 0:2304], w_in[..., 2368:2880], w_in[..., 2304:2368],
        jnp.zeros((L, D, 192), w_in.dtype), w_in[..., 2880:o_gl]], axis=-1).astype(BF16)
    assert w_in_p.shape[-1] == off + _SEC_TOTAL
    cond = jnp.zeros((COND_ROWS, D), F32).at[0].set(c_ctx).at[1:1 + Bs].set(c)
    wuq_p = jnp.pad(w_uq, ((0, 0), (0, 0), (0, 0), (0, MLA_NOPE + MLA_VD - MLA_QK))).reshape(
        L, MLA_Q_RANK, -1).astype(BF16)
    wukv_p = w_ukv.reshape(L, MLA_KV_RANK, -1).astype(BF16)
    pad_qk = ((0, 0), (0, MLA_NOPE + MLA_VD - MLA_QK))
    mla_w = (wuq_p, wukv_p, mla_q_gain[:, None, :], mla_kv_gain[:, None, :],
             jnp.pad(mla_qn_gain, pad_qk)[:, None, :], jnp.pad(mla_kn_gain, pad_qk)[:, None, :])
    diff_g = (jnp.tile(diff_q_gain, (1, 2))[:, None, :], jnp.tile(diff_k_gain, (1, 2))[:, None, :],
              diff_subln[:, None, :])
    wd_b, wm_b, wr_b, wo_b = (w.astype(BF16) for w in (w_br_diff, w_br_mla, w_br_ret, w_o))
    w_router_p = jnp.pad(w_router, ((0, 0), (0, 0), (0, LANES - E)))
    b_router_p = jnp.pad(b_router, ((0, 0), (0, LANES - E)))[:, None, :]
    lam_vec = diff_lambda.astype(F32)
    lam_init = jnp.asarray([0.8 - 0.6 * math.exp(-0.3 * l) for l in range(L)], F32)
    lam = (jnp.exp(jnp.sum(lam_vec[:, 0] * lam_vec[:, 1], -1))
           - jnp.exp(jnp.sum(lam_vec[:, 2] * lam_vec[:, 3], -1)) + lam_init)
    log_g = jax.nn.log_sigmoid(ret_decay.astype(F32)).reshape(-1)
    cdk = cache_diff_k.reshape(Bs, L, P, DIFF_HEADS * 2 * DIFF_HD)
    cdv = cache_diff_v.reshape(Bs, L, P, DIFF_HEADS * DIFF_VD)
    ckr = jnp.pad(cache_mla_krope, ((0, 0), (0, 0), (0, 0), (0, LANES - MLA_ROPE)))
    t_pos = jnp.arange(Ss, dtype=jnp.int32)
    tabs_ax = _rope_tables(t_pos // GRID_W, t_pos % GRID_W, 16, 2)
    tabs_1d = _rope_tables(t_pos, t_pos, 32, 1)
    bg4, bu4, bd4 = b_gate[:, :, None, :], b_up[:, :, None, :], b_down[:, :, None, :]

    tm_p = _pick_tile(1024, Tp, Ss)
    tm_r = ROW_ALIGN
    tq = ROW_ALIGN
    rows_sorted = T * TOP_K + E * EXPERT_CHUNK
    G = E + (T * TOP_K) // (GROUP_CHUNKS * EXPERT_CHUNK)
    tf = 512

    mod = _mod_call(cond, w_mod, b_mod)
    x = jnp.concatenate([x_prompt.reshape(Tp, D), x_sample.reshape(Ts, D)], axis=0)
    outs = [[] for _ in range(6)]
    for l in range(L):
        modt_p = mod[l][_cond_tiles(T, Tp, Ss, tm_p)][:, None, :]
        modt_r = mod[l][_cond_tiles(T, Tp, Ss, tm_r)][:, None, :]
        proj = _proj_call(l, x, modt_p, norm1[:, None, :], w_in_p, tm_p, 512)

        yd_p, kn_p = _diff_call(l, proj, off, lam, diff_g, Bp, Sp, 0, min(tq, Sp), False)
        (yd_s,) = _diff_call(l, proj, off, lam, diff_g, Bs, Ss, Tp, tq, True, (cdk, cdv), tabs_ax)
        ym_p, ckv_p = _mla_call(l, proj, off, mla_w, Bp, Sp, 0, min(tq, Sp), False)
        (ym_s,) = _mla_call(l, proj, off, mla_w, Bs, Ss, Tp, tq, True, (cache_mla_ckv, ckr), tabs_ax)
        yr_p, sf_p, sb_p = _ret_call(l, proj, off, log_g, Bp, Sp, 0, False)
        (yr_s,) = _ret_call(l, proj, off, log_g, Bs, Ss, Tp, True, (state_ret_fwd, state_ret_bwd), tabs_1d)

        yd = jnp.concatenate([yd_p, yd_s], axis=0)
        ym = jnp.concatenate([ym_p, ym_s], axis=0)
        yr = jnp.concatenate([yr_p, yr_s], axis=0)
        merged = _merge_call(l, yd, ym, yr, proj, wd_b, wm_b, wr_b, tm_r)
        x1, h2, route, cnt = _post_call(l, merged, x, wo_b, modt_r, norm2[:, None, :],
                                        w_router_p, b_router_p, E, tm_r)
        dest, pstart, pcnt, ge, gs, gn = _route_plan(route, cnt, E, G)
        xs = _dispatch_call(h2, dest, pstart, pcnt, rows_sorted, E)
        os_ = _expert_call(l, xs, ge, gs, gn, w_gate, w_up, w_down, bg4, bu4, bd4, tf)
        x = _combine_call(dest, os_, x1, modt_r, route, tm_r)

        dv = proj[:Tp, off + _SEC["DV"]:off + _SEC["DV"] + DIFF_HEADS * DIFF_VD]
        kr = proj[:Tp, off + _SEC["MKR"]:off + _SEC["MKR"] + MLA_ROPE]
        for lst, v in zip(outs, (kn_p.reshape(Bp, Sp, DIFF_HEADS, 2 * DIFF_HD),
                                 dv.reshape(Bp, Sp, DIFF_HEADS, DIFF_VD),
                                 ckv_p.reshape(Bp, Sp, MLA_KV_RANK), kr.reshape(Bp, Sp, MLA_ROPE),
                                 sf_p, sb_p)):
            lst.append(v)

    y_prompt = x[:Tp].reshape(Bp, Sp, D)
    y_sample = x[Tp:].reshape(Bs, Ss, D)
    return (y_prompt, y_sample) + tuple(jnp.stack(o, axis=1) for o in outs)
```

```python
import functools
import math

import jax
import jax.numpy as jnp
import numpy as np
from jax import lax
from jax.experimental import pallas as pl
from jax.experimental.pallas import tpu as pltpu

F32 = jnp.float32
BF16 = jnp.bfloat16

GRID_W = 64
DIFF_HEADS = 4
DIFF_HD = 64
DIFF_VD = 128
MLA_HEADS = 8
MLA_Q_RANK = 512
MLA_KV_RANK = 256
MLA_NOPE = 128
MLA_ROPE = 64
MLA_VD = 128
MLA_QK = MLA_NOPE + MLA_ROPE
RET_HEADS = 4
RET_QK = 64
RET_VD = 128
RET_CHUNK = 128
TOP_K = 4
SWIGLU_LIMIT = 7.0
SWIGLU_ALPHA = 1.702
ROPE_BASE = 10000.0
EPS = 1e-6

LANES = 128
ROW_ALIGN = 256
EXPERT_CHUNK = 256
GROUP_CHUNKS = 4
COND_ROWS = 8
VMEM_LIMIT = 56 * 1024 * 1024

_SEC = dict(DQ=0, DK=512, DV=1024, MQ=1536, MKV=2048, RQ=2304, RK=2560, MKR=2816,
            RV=3072, RGF=3584, RGB=4096)
_SEC_TOTAL = 4608


def _cp(sem, vmem=VMEM_LIMIT):
    return pltpu.CompilerParams(dimension_semantics=sem, vmem_limit_bytes=vmem)


def _lane_iota(shape):
    return lax.broadcasted_iota(jnp.int32, shape, len(shape) - 1)


def _rms(x, n=None):
    n = x.shape[-1] if n is None else n
    return x * lax.rsqrt(jnp.sum(x * x, axis=-1, keepdims=True) / n + EPS)


def _dot(a, b):
    return jnp.dot(a, b, preferred_element_type=F32)


def _dot_nt(a, b):
    return lax.dot_general(a, b, (((1,), (1,)), ((), ())), preferred_element_type=F32)


def _softmax(s):
    e = jnp.exp(s - jnp.max(s, axis=-1, keepdims=True))
    return e / jnp.sum(e, axis=-1, keepdims=True)


def _sigmoid(x):
    return 1.0 / (1.0 + jnp.exp(-x))


def _rope(x, cos, sin, half):
    lane = _lane_iota(x.shape)
    up = pltpu.roll(x, LANES - half, 1)
    dn = pltpu.roll(x, half, 1)
    sw = jnp.where(lane % (2 * half) < half, up, dn)
    return x * cos + sw * sin


def _norm_halves(x, gain):
    lane = _lane_iota(x.shape)
    lo = lane < DIFF_HD
    x2 = x * x
    ss_lo = jnp.sum(jnp.where(lo, x2, 0.0), axis=-1, keepdims=True)
    ss_hi = jnp.sum(jnp.where(lo, 0.0, x2), axis=-1, keepdims=True)
    inv = jnp.where(lo, lax.rsqrt(ss_lo / DIFF_HD + EPS), lax.rsqrt(ss_hi / DIFF_HD + EPS))
    return x * inv * gain


def _mod_kernel(c_ref, w_ref, b_ref, o_ref):
    c = c_ref[...]
    a = (c * _sigmoid(c)).astype(BF16)
    o_ref[...] = _dot(a, w_ref[...].astype(BF16)) + b_ref[...]


def _mod_call(cond, w_mod, b_mod):
    L, D, N = w_mod.shape
    tn = 1024 if N % 1024 == 0 else 512
    return pl.pallas_call(
        _mod_kernel,
        grid=(L, N // tn),
        in_specs=[
            pl.BlockSpec((COND_ROWS, D), lambda l, j: (0, 0)),
            pl.BlockSpec((None, D, tn), lambda l, j: (l, 0, j)),
            pl.BlockSpec((None, 1, tn), lambda l, j: (l, 0, j)),
        ],
        out_specs=pl.BlockSpec((None, COND_ROWS, tn), lambda l, j: (l, 0, j)),
        out_shape=jax.ShapeDtypeStruct((L, COND_ROWS, N), F32),
        compiler_params=_cp(("arbitrary", "arbitrary")),
        name="adaln_mod",
    )(cond, w_mod, b_mod.reshape(L, 1, N))


def _proj_kernel(x_ref, sc_ref, sh_ref, g_ref, w_ref, o_ref, h_ref):
    @pl.when(pl.program_id(1) == 0)
    def _():
        h = _rms(x_ref[...]) * g_ref[...]
        h_ref[...] = (h * (1.0 + sc_ref[...]) + sh_ref[...]).astype(BF16)

    o_ref[...] = _dot(h_ref[...], w_ref[...])


def _proj_call(l, x, modt, norm1, w_in_p, tm, tn):
    T, D = x.shape
    PC = w_in_p.shape[-1]
    return pl.pallas_call(
        _proj_kernel,
        grid=(T // tm, PC // tn),
        in_specs=[
            pl.BlockSpec((tm, D), lambda i, j: (i, 0)),
            pl.BlockSpec((None, 1, D), lambda i, j: (i, 0, 1)),
            pl.BlockSpec((None, 1, D), lambda i, j: (i, 0, 0)),
            pl.BlockSpec((None, 1, D), lambda i, j: (l, 0, 0)),
            pl.BlockSpec((None, D, tn), lambda i, j: (l, 0, j)),
        ],
        out_specs=pl.BlockSpec((tm, tn), lambda i, j: (i, j)),
        out_shape=jax.ShapeDtypeStruct((T, PC), F32),
        scratch_shapes=[pltpu.VMEM((tm, D), BF16)],
        compiler_params=_cp(("arbitrary", "arbitrary")),
        name="norm_proj",
    )(x, modt, modt, norm1, w_in_p)


def _diff_kernel(*refs, latent, S, P, lam_init, l):
    if latent:
        (lam_ref, q_ref, k_ref, v_ref, qg_ref, kg_ref, sg_ref, ck_ref, cv_ref,
         cosq_ref, sinq_ref, cosk_ref, sink_ref, y_ref, kall, vall) = refs
    else:
        (lam_ref, q_ref, k_ref, v_ref, qg_ref, kg_ref, sg_ref, y_ref, kn_ref, kall, vall) = refs

    @pl.when(pl.program_id(1) == 0)
    def _():
        for h in range(DIFF_HEADS):
            sl = slice(LANES * h, LANES * (h + 1))
            kn = _norm_halves(k_ref[:, sl], kg_ref[...])
            if latent:
                kn = _rope(kn, cosk_ref[...], sink_ref[...], 16)
            else:
                kn_ref[:, sl] = kn
            kall[0:S, sl] = kn.astype(BF16)
        vall[0:S, :] = v_ref[...].astype(BF16)
        if latent:
            kall[S:S + P, :] = ck_ref[...].astype(BF16)
            vall[S:S + P, :] = cv_ref[...].astype(BF16)

    lam = lam_ref[l]
    scale = DIFF_HD ** -0.5
    for h in range(DIFF_HEADS):
        sl = slice(LANES * h, LANES * (h + 1))
        qn = _norm_halves(q_ref[:, sl], qg_ref[...])
        if latent:
            qn = _rope(qn, cosq_ref[...], sinq_ref[...], 16)
        lo = _lane_iota(qn.shape) < DIFF_HD
        q1 = jnp.where(lo, qn, 0.0).astype(BF16)
        q2 = jnp.where(lo, 0.0, qn).astype(BF16)
        kh = kall[:, sl]
        p = _softmax(_dot_nt(q1, kh) * scale) - lam * _softmax(_dot_nt(q2, kh) * scale)
        o = _dot(p.astype(BF16), vall[:, sl])
        y_ref[:, sl] = (_rms(o) * sg_ref[...] * (1.0 - lam_init)).astype(BF16)


def _diff_call(l, proj, off, lam, gains, B, S, row0, tq, latent, cache=None, tabs=None):
    qg, kg, sg = gains
    W = DIFF_HEADS * LANES
    nq = S // tq
    P = cache[0].shape[2] if latent else 0
    lam_init = 0.8 - 0.6 * math.exp(-0.3 * l)
    cq, ck, cv = ((off + _SEC[n]) // W for n in ("DQ", "DK", "DV"))
    in_specs = [
        pl.BlockSpec(memory_space=pltpu.SMEM),
        pl.BlockSpec((tq, W), lambda b, i: (row0 // tq + b * nq + i, cq)),
        pl.BlockSpec((S, W), lambda b, i: (row0 // S + b, ck)),
        pl.BlockSpec((S, W), lambda b, i: (row0 // S + b, cv)),
        pl.BlockSpec((None, 1, LANES), lambda b, i: (l, 0, 0)),
        pl.BlockSpec((None, 1, LANES), lambda b, i: (l, 0, 0)),
        pl.BlockSpec((None, 1, LANES), lambda b, i: (l, 0, 0)),
    ]
    args = [lam, proj, proj, proj, qg, kg, sg]
    out_shape = [jax.ShapeDtypeStruct((B * S, W), BF16)]
    out_specs = [pl.BlockSpec((tq, W), lambda b, i: (b * nq + i, 0))]
    if latent:
        cos, sin = tabs
        in_specs += [
            pl.BlockSpec((None, None, P, W), lambda b, i: (b, l, 0, 0)),
            pl.BlockSpec((None, None, P, W), lambda b, i: (b, l, 0, 0)),
            pl.BlockSpec((tq, LANES), lambda b, i: (i, 0)),
            pl.BlockSpec((tq, LANES), lambda b, i: (i, 0)),
            pl.BlockSpec((S, LANES), lambda b, i: (0, 0)),
            pl.BlockSpec((S, LANES), lambda b, i: (0, 0)),
        ]
        args += [cache[0], cache[1], cos, sin, cos, sin]
    else:
        out_shape.append(jax.ShapeDtypeStruct((B * S, W), F32))
        out_specs.append(pl.BlockSpec((S, W), lambda b, i: (b, 0)))
    return pl.pallas_call(
        functools.partial(_diff_kernel, latent=latent, S=S, P=P, lam_init=lam_init, l=l),
        grid=(B, nq),
        in_specs=in_specs,
        out_specs=out_specs,
        out_shape=out_shape,
        scratch_shapes=[pltpu.VMEM((S + P, W), BF16), pltpu.VMEM((S + P, W), BF16)],
        compiler_params=_cp(("arbitrary", "arbitrary")),
        name="diff_attn_lat" if latent else "diff_attn_ctx",
    )(*args)


def _mla_kernel(*refs, latent, S, P):
    if latent:
        (mq_ref, mkv_ref, mkr_ref, wuq_ref, wukv_ref, qg_ref, kvg_ref, qng_ref, kng_ref,
         cckv_ref, ckr_ref, cosq_ref, sinq_ref, cosk_ref, sink_ref, y_ref, kall, vall) = refs
    else:
        (mq_ref, mkv_ref, mkr_ref, wuq_ref, wukv_ref, qg_ref, kvg_ref, qng_ref, kng_ref,
         y_ref, ckv_ref, kall, vall) = refs
    HW = MLA_NOPE + MLA_VD

    @pl.when(pl.program_id(1) == 0)
    def _():
        ckv = _rms(mkv_ref[...]) * kvg_ref[...]
        if not latent:
            ckv_ref[...] = ckv
        segs = [(0, S, ckv, mkr_ref[...], latent)]
        if latent:
            segs.append((S, P, cckv_ref[...], ckr_ref[...], False))
        for r0, R, cv, kr, rot in segs:
            kv = _dot(cv.astype(BF16), wukv_ref[...])
            kr_ss = jnp.sum(kr * kr, axis=-1, keepdims=True)
            krg = kr * kng_ref[:, MLA_NOPE:]
            if rot:
                krg = _rope(krg, cosk_ref[...], sink_ref[...], 16)
            for h in range(MLA_HEADS):
                kn = kv[:, HW * h:HW * h + MLA_NOPE]
                ss = jnp.sum(kn * kn, axis=-1, keepdims=True) + kr_ss
                r = lax.rsqrt(ss / MLA_QK + EPS)
                kall[h, r0:r0 + R, 0:MLA_NOPE] = (kn * r * kng_ref[:, 0:MLA_NOPE]).astype(BF16)
                kall[h, r0:r0 + R, MLA_NOPE:] = (krg * r).astype(BF16)
                vall[h, r0:r0 + R, :] = kv[:, HW * h + MLA_NOPE:HW * (h + 1)].astype(BF16)

    cq = (_rms(mq_ref[...]) * qg_ref[...]).astype(BF16)
    qm = _dot(cq, wuq_ref[...])
    scale = MLA_QK ** -0.5
    for h in range(MLA_HEADS):
        qn = _rms(qm[:, HW * h:HW * (h + 1)], MLA_QK) * qng_ref[...]
        if latent:
            qr = _rope(qn[:, MLA_NOPE:], cosq_ref[...], sinq_ref[...], 16)
            qn = jnp.concatenate([qn[:, 0:MLA_NOPE], qr], axis=-1)
        p = _softmax(_dot_nt(qn.astype(BF16), kall[h]) * scale)
        y_ref[:, MLA_VD * h:MLA_VD * (h + 1)] = _dot(p.astype(BF16), vall[h]).astype(BF16)


def _mla_call(l, proj, off, weights, B, S, row0, tq, latent, cache=None, tabs=None):
    wuq, wukv, qg, kvg, qng, kng = weights
    nq = S // tq
    P = cache[0].shape[2] if latent else 0
    HW = MLA_NOPE + MLA_VD
    NW = MLA_HEADS * HW
    c_mq = (off + _SEC["MQ"]) // MLA_Q_RANK
    c_mkv = (off + _SEC["MKV"]) // MLA_KV_RANK
    c_mkr = (off + _SEC["MKR"]) // LANES

    def lay(shape):
        return pl.BlockSpec((None,) + shape, lambda b, i: (l,) + (0,) * len(shape))

    in_specs = [
        pl.BlockSpec((tq, MLA_Q_RANK), lambda b, i: (row0 // tq + b * nq + i, c_mq)),
        pl.BlockSpec((S, MLA_KV_RANK), lambda b, i: (row0 // S + b, c_mkv)),
        pl.BlockSpec((S, LANES), lambda b, i: (row0 // S + b, c_mkr)),
        lay((MLA_Q_RANK, NW)), lay((MLA_KV_RANK, NW)),
        lay((1, MLA_Q_RANK)), lay((1, MLA_KV_RANK)), lay((1, HW)), lay((1, HW)),
    ]
    args = [proj, proj, proj, wuq, wukv, qg, kvg, qng, kng]
    out_shape = [jax.ShapeDtypeStruct((B * S, MLA_HEADS * MLA_VD), BF16)]
    out_specs = [pl.BlockSpec((tq, MLA_HEADS * MLA_VD), lambda b, i: (b * nq + i, 0))]
    if latent:
        cos, sin = tabs
        in_specs += [
            pl.BlockSpec((None, None, P, MLA_KV_RANK), lambda b, i: (b, l, 0, 0)),
            pl.BlockSpec((None, None, P, LANES), lambda b, i: (b, l, 0, 0)),
            pl.BlockSpec((tq, LANES), lambda b, i: (i, 0)),
            pl.BlockSpec((tq, LANES), lambda b, i: (i, 0)),
            pl.BlockSpec((S, LANES), lambda b, i: (0, 0)),
            pl.BlockSpec((S, LANES), lambda b, i: (0, 0)),
        ]
        args += [cache[0], cache[1], cos, sin, cos, sin]
    else:
        out_shape.append(jax.ShapeDtypeStruct((B * S, MLA_KV_RANK), F32))
        out_specs.append(pl.BlockSpec((S, MLA_KV_RANK), lambda b, i: (b, 0)))
    return pl.pallas_call(
        functools.partial(_mla_kernel, latent=latent, S=S, P=P),
        grid=(B, nq),
        in_specs=in_specs,
        out_specs=out_specs,
        out_shape=out_shape,
        scratch_shapes=[pltpu.VMEM((MLA_HEADS, S + P, HW), BF16),
                        pltpu.VMEM((MLA_HEADS, S + P, MLA_VD), BF16)],
        compiler_params=_cp(("arbitrary", "arbitrary")),
        name="mla_lat" if latent else "mla_ctx",
    )(*args)


def _ret_kernel(*refs, latent, S, l):
    if latent:
        (lg_ref, rq_ref, rk_ref, rv_ref, gf_ref, gb_ref, cos_ref, sin_ref, s0f_ref, s0b_ref,
         y_ref, qs, ks, of, ob) = refs
    else:
        (lg_ref, rq_ref, rk_ref, rv_ref, gf_ref, gb_ref, y_ref, sf_ref, sb_ref, qs, ks, of, ob) = refs
    C = RET_CHUNK
    nc = S // C
    ii = lax.broadcasted_iota(jnp.int32, (C, C), 0).astype(F32)
    jj = lax.broadcasted_iota(jnp.int32, (C, C), 1).astype(F32)
    col = lax.broadcasted_iota(jnp.int32, (C, 1), 0).astype(F32)
    one = jnp.ones((1, 1), F32)
    for s in range(RET_HEADS // 2):
        ssl = slice(LANES * s, LANES * (s + 1))
        qslab = rq_ref[:, ssl]
        kslab = rk_ref[:, ssl] * (RET_QK ** -0.5)
        if latent:
            qslab = _rope(qslab, cos_ref[...], sin_ref[...], 32)
            kslab = _rope(kslab, cos_ref[...], sin_ref[...], 32)
        lo = _lane_iota(qslab.shape) < RET_QK
        for u in range(2):
            h = 2 * s + u
            vsl = slice(RET_VD * h, RET_VD * (h + 1))
            keep = lo if u == 0 else jnp.logical_not(lo)
            qs[...] = jnp.where(keep, qslab, 0.0)
            ks[...] = jnp.where(keep, kslab, 0.0)
            for d in range(2):
                lg = lg_ref[(l * 2 + d) * RET_HEADS + h]
                if d == 0:
                    rel = ii - jj
                    qdec = jnp.exp((col + 1.0) * lg)
                    kdec = jnp.exp((C - 1.0 - col) * lg)
                else:
                    rel = jj - ii
                    qdec = jnp.exp((C - col) * lg)
                    kdec = jnp.exp(col * lg)
                dmat = jnp.where(rel >= 0, jnp.exp(jnp.maximum(rel, 0.0) * lg), 0.0)
                cdec = jnp.exp(one * (C * lg))
                o_ref = of if d == 0 else ob
                if latent:
                    s0 = (s0f_ref if d == 0 else s0b_ref)[h]
                    z = jnp.zeros_like(s0)
                    st0 = jnp.concatenate([s0, z] if u == 0 else [z, s0], axis=0)
                else:
                    st0 = jnp.zeros((LANES, RET_VD), F32)

                def body(n, st, d=d, dmat=dmat, qdec=qdec, kdec=kdec, cdec=cdec, o_ref=o_ref, vsl=vsl):
                    cidx = n if d == 0 else nc - 1 - n
                    r0 = pl.multiple_of(cidx * C, C)
                    qc = qs[pl.ds(r0, C), :]
                    kc = ks[pl.ds(r0, C), :]
                    vc = rv_ref[pl.ds(r0, C), vsl].astype(BF16)
                    sc = _dot_nt(qc.astype(BF16), kc.astype(BF16)) * dmat
                    o = _dot(sc.astype(BF16), vc) + _dot((qc * qdec).astype(BF16), st.astype(BF16))
                    o_ref[pl.ds(r0, C), :] = o
                    kt = jnp.transpose(kc * kdec).astype(BF16)
                    return st * cdec + _dot(kt, vc)

                st = lax.fori_loop(0, nc, body, st0)
                if not latent:
                    (sf_ref if d == 0 else sb_ref)[h] = st[RET_QK * u:RET_QK * (u + 1), :]
            gf = gf_ref[:, vsl]
            gb = gb_ref[:, vsl]
            y = _rms(of[...]) * (gf * _sigmoid(gf)) + _rms(ob[...]) * (gb * _sigmoid(gb))
            y_ref[:, vsl] = y.astype(BF16)


def _ret_call(l, proj, off, lg, B, S, row0, latent, states=None, tabs=None):
    QW = RET_HEADS * RET_QK
    VW = RET_HEADS * RET_VD
    c_rq, c_rk = (off + _SEC["RQ"]) // QW, (off + _SEC["RK"]) // QW
    c_rv, c_gf, c_gb = ((off + _SEC[n]) // VW for n in ("RV", "RGF", "RGB"))
    rb = row0 // S
    in_specs = [
        pl.BlockSpec(memory_space=pltpu.SMEM),
        pl.BlockSpec((S, QW), lambda b: (rb + b, c_rq)),
        pl.BlockSpec((S, QW), lambda b: (rb + b, c_rk)),
        pl.BlockSpec((S, VW), lambda b: (rb + b, c_rv)),
        pl.BlockSpec((S, VW), lambda b: (rb + b, c_gf)),
        pl.BlockSpec((S, VW), lambda b: (rb + b, c_gb)),
    ]
    args = [lg, proj, proj, proj, proj, proj]
    out_shape = [jax.ShapeDtypeStruct((B * S, VW), BF16)]
    out_specs = [pl.BlockSpec((S, VW), lambda b: (b, 0))]
    st_block = (None, RET_HEADS, RET_QK, RET_VD)
    if latent:
        cos, sin = tabs
        in_specs += [
            pl.BlockSpec((S, LANES), lambda b: (0, 0)),
            pl.BlockSpec((S, LANES), lambda b: (0, 0)),
            pl.BlockSpec((None,) + st_block, lambda b: (b, l, 0, 0, 0)),
            pl.BlockSpec((None,) + st_block, lambda b: (b, l, 0, 0, 0)),
        ]
        args += [cos, sin, states[0], states[1]]
    else:
        for _ in range(2):
            out_shape.append(jax.ShapeDtypeStruct((B, RET_HEADS, RET_QK, RET_VD), F32))
            out_specs.append(pl.BlockSpec(st_block, lambda b: (b, 0, 0, 0)))
    return pl.pallas_call(
        functools.partial(_ret_kernel, latent=latent, S=S, l=l),
        grid=(B,),
        in_specs=in_specs,
        out_specs=out_specs,
        out_shape=out_shape,
        scratch_shapes=[pltpu.VMEM((S, LANES), F32), pltpu.VMEM((S, LANES), F32),
                        pltpu.VMEM((S, RET_VD), F32), pltpu.VMEM((S, RET_VD), F32)],
        compiler_params=_cp(("arbitrary",)),
        name="retention_lat" if latent else "retention_ctx",
    )(*args)


def _merge_kernel(yd_ref, ym_ref, yr_ref, g0_ref, g1_ref, g2_ref, wd_ref, wm_ref, wr_ref, o_ref):
    acc = _sigmoid(g0_ref[...]) * _dot(yd_ref[...], wd_ref[...])
    acc = acc + _sigmoid(g1_ref[...]) * _dot(ym_ref[...], wm_ref[...])
    acc = acc + _sigmoid(g2_ref[...]) * _dot(yr_ref[...], wr_ref[...])
    o_ref[...] = acc.astype(BF16)


def _merge_call(l, yd, ym, yr, proj, wd, wm, wr, tm):
    T = yd.shape[0]
    D = wd.shape[-1]

    def rows(w):
        return pl.BlockSpec((tm, w), lambda i: (i, 0))

    def gate(k):
        return pl.BlockSpec((tm, D), lambda i: (i, k))

    def wt(w):
        return pl.BlockSpec((None, w.shape[1], D), lambda i: (l, 0, 0))

    return pl.pallas_call(
        _merge_kernel,
        grid=(T // tm,),
        in_specs=[rows(yd.shape[1]), rows(ym.shape[1]), rows(yr.shape[1]),
                  gate(0), gate(1), gate(2), wt(wd), wt(wm), wt(wr)],
        out_specs=pl.BlockSpec((tm, D), lambda i: (i, 0)),
        out_shape=jax.ShapeDtypeStruct((T, D), BF16),
        compiler_params=_cp(("arbitrary",)),
        name="branch_merge",
    )(yd, ym, yr, proj, proj, proj, wd, wm, wr)


def _post_kernel(m_ref, x_ref, wo_ref, g1_ref, sc_ref, sh_ref, n2_ref, wr_ref, br_ref,
                 x1_ref, h2_ref, route_ref, cnt_ref, run_ref, *, E, tm):
    @pl.when(pl.program_id(0) == 0)
    def _():
        run_ref[...] = jnp.zeros_like(run_ref)

    x1 = x_ref[...] + g1_ref[...] * _dot(m_ref[...], wo_ref[...])
    x1_ref[...] = x1
    h = _rms(x1) * n2_ref[...]
    h = h * (1.0 + sc_ref[...]) + sh_ref[...]
    h2_ref[...] = h
    logits = jnp.dot(h, wr_ref[...], preferred_element_type=F32,
                     precision=lax.Precision.HIGHEST) + br_ref[...]
    lane = _lane_iota(logits.shape)
    cur = jnp.where(lane < E, logits, -jnp.inf)
    hots, vals = [], []
    for _ in range(TOP_K):
        m = jnp.max(cur, axis=-1, keepdims=True)
        idx = jnp.min(jnp.where(cur == m, lane, LANES), axis=-1, keepdims=True)
        hot = lane == idx
        hots.append(hot)
        vals.append(m)
        cur = jnp.where(hot, -jnp.inf, cur)
    exps = [jnp.exp(v - vals[0]) for v in vals]
    den = exps[0] + exps[1] + exps[2] + exps[3]
    sel = jnp.zeros(logits.shape, F32)
    for hot in hots:
        sel = sel + jnp.where(hot, 1.0, 0.0)
    ri = lax.broadcasted_iota(jnp.int32, (tm, tm), 0)
    ci = lax.broadcasted_iota(jnp.int32, (tm, tm), 1)
    tri = jnp.where(ci < ri, 1.0, 0.0).astype(BF16)
    rank = _dot(tri, sel.astype(BF16)) + run_ref[0:1, :]
    lane_f = lane.astype(F32)
    route = jnp.zeros(logits.shape, F32)
    for k in range(TOP_K):
        e_k = jnp.sum(jnp.where(hots[k], lane_f, 0.0), axis=-1, keepdims=True)
        r_k = jnp.sum(jnp.where(hots[k], rank, 0.0), axis=-1, keepdims=True)
        route = jnp.where(lane == k, e_k, route)
        route = jnp.where(lane == TOP_K + k, r_k, route)
        route = jnp.where(lane == 2 * TOP_K + k, exps[k] / den, route)
    route_ref[...] = route
    total = run_ref[...] + jnp.sum(sel, axis=0, keepdims=True)
    run_ref[...] = total
    cnt_ref[...] = total


def _post_call(l, merged, x, wo, modt, norm2, w_router, b_router, E, tm):
    T, D = x.shape

    def modc(k):
        return pl.BlockSpec((None, 1, D), lambda i: (i, 0, k))

    return pl.pallas_call(
        functools.partial(_post_kernel, E=E, tm=tm),
        grid=(T // tm,),
        in_specs=[
            pl.BlockSpec((tm, D), lambda i: (i, 0)),
            pl.BlockSpec((tm, D), lambda i: (i, 0)),
            pl.BlockSpec((None, D, D), lambda i: (l, 0, 0)),
            modc(2), modc(4), modc(3),
            pl.BlockSpec((None, 1, D), lambda i: (l, 0, 0)),
            pl.BlockSpec((None, D, LANES), lambda i: (l, 0, 0)),
            pl.BlockSpec((None, 1, LANES), lambda i: (l, 0, 0)),
        ],
        out_specs=[
            pl.BlockSpec((tm, D), lambda i: (i, 0)),
            pl.BlockSpec((tm, D), lambda i: (i, 0)),
            pl.BlockSpec((tm, LANES), lambda i: (i, 0)),
            pl.BlockSpec((COND_ROWS, LANES), lambda i: (0, 0)),
        ],
        out_shape=[
            jax.ShapeDtypeStruct((T, D), F32),
            jax.ShapeDtypeStruct((T, D), F32),
            jax.ShapeDtypeStruct((T, LANES), F32),
            jax.ShapeDtypeStruct((COND_ROWS, LANES), F32),
        ],
        scratch_shapes=[pltpu.VMEM((COND_ROWS, LANES), F32)],
        compiler_params=_cp(("arbitrary",)),
        name="out_proj_router",
    )(merged, x, wo, modt, modt, modt, norm2, w_router, b_router)


def _dispatch_kernel(dest_ref, pstart_ref, pcnt_ref, h_ref, xs_hbm, sem, *, tm, E):
    i = pl.program_id(0)

    def row_copy(r, dst):
        return pltpu.make_async_copy(h_ref.at[pl.ds(r, 1)], xs_hbm.at[pl.ds(dst, 1)], sem)

    def body(r, c):
        t = i * tm + r
        for k in range(TOP_K):
            row_copy(r, dest_ref[t * TOP_K + k]).start()
        return c

    lax.fori_loop(0, tm, body, 0)

    @pl.when(i == 0)
    def _():
        def pad_body(e, c):
            s = pstart_ref[e]
            n = pcnt_ref[e]
            lax.fori_loop(0, n, lambda r, c2: (row_copy(0, s + r).start(), c2)[1], 0)
            lax.fori_loop(0, n, lambda r, c2: (row_copy(0, 0).wait(), c2)[1], 0)
            return c

        lax.fori_loop(0, E, pad_body, 0)

    for _ in range(TOP_K):
        pltpu.make_async_copy(h_ref, xs_hbm.at[pl.ds(0, tm)], sem).wait()


def _dispatch_call(h2, dest, pstart, pcnt, rows, E, tm):
    T, D = h2.shape
    return pl.pallas_call(
        functools.partial(_dispatch_kernel, tm=tm, E=E),
        grid_spec=pltpu.PrefetchScalarGridSpec(
            num_scalar_prefetch=3,
            grid=(T // tm,),
            in_specs=[pl.BlockSpec((tm, D), lambda i, d, ps, pc: (i, 0))],
            out_specs=pl.BlockSpec(memory_space=pl.ANY),
            scratch_shapes=[pltpu.SemaphoreType.DMA],
        ),
        out_shape=jax.ShapeDtypeStruct((rows, D), F32),
        compiler_params=_cp(("arbitrary",)),
        name="expert_dispatch",
    )(dest, pstart, pcnt, h2)


def _expert_kernel(ge_ref, gs_ref, gn_ref, xs_hbm, wg_ref, wu_ref, wd_ref, bg_ref, bu_ref, bd_ref,
                   os_hbm, xb, ybuf, wgb, wub, wdb, sem_in, sem_out, *, NF):
    g = pl.program_id(0)
    f = pl.program_id(1)
    n = gn_ref[g]
    start = pl.multiple_of(gs_ref[g], EXPERT_CHUNK)

    def chunk_rows(c):
        return pl.ds(pl.multiple_of(start + c * EXPERT_CHUNK, EXPERT_CHUNK), EXPERT_CHUNK)

    def x_copy(c):
        return pltpu.make_async_copy(xs_hbm.at[chunk_rows(c)], ybuf.at[c], sem_in.at[c])

    def y_copy(c):
        return pltpu.make_async_copy(ybuf.at[c], os_hbm.at[chunk_rows(c)], sem_out.at[c])

    def for_chunks(fn, first=0):
        for c in range(first, GROUP_CHUNKS):
            pl.when(c < n)(functools.partial(fn, c))

    def load_chunk(c):
        x_copy(c).wait()
        xb[c] = ybuf[c].astype(BF16)
        ybuf[c] = jnp.broadcast_to(bd_ref[...], ybuf.shape[1:])

    def compute_chunk(c):
        x = xb[c]
        gt = jnp.minimum(_dot(x, wgb[...]) + bg_ref[...], SWIGLU_LIMIT)
        up = jnp.clip(_dot(x, wub[...]) + bu_ref[...], -SWIGLU_LIMIT, SWIGLU_LIMIT)
        act = ((up + 1.0) * gt * _sigmoid(SWIGLU_ALPHA * gt)).astype(BF16)
        ybuf[c] += _dot(act, wdb[...])

    @pl.when(n > 0)
    def _():
        @pl.when(f == 0)
        def _():
            for_chunks(lambda c: x_copy(c).start())

        wgb[...] = wg_ref[...].astype(BF16)
        wub[...] = wu_ref[...].astype(BF16)
        wdb[...] = wd_ref[...].astype(BF16)

        @pl.when(f == 0)
        def _():
            for_chunks(load_chunk)

        compute_chunk(0)
        for_chunks(compute_chunk, first=1)

        @pl.when(f == NF - 1)
        def _():
            for_chunks(lambda c: y_copy(c).start())
            for_chunks(lambda c: y_copy(c).wait())


def _expert_call(l, xs, ge, gs, gn, w_gate, w_up, w_down, b_gate, b_up, b_down, tf):
    rows, D = xs.shape
    F = w_gate.shape[-1]
    NF = F // tf
    G = ge.shape[0]

    def fsel(f, gn, g):
        return jnp.where(gn[g] > 0, f, NF - 1)

    return pl.pallas_call(
        functools.partial(_expert_kernel, NF=NF),
        grid_spec=pltpu.PrefetchScalarGridSpec(
            num_scalar_prefetch=3,
            grid=(G, NF),
            in_specs=[
                pl.BlockSpec(memory_space=pl.ANY),
                pl.BlockSpec((None, None, D, tf), lambda g, f, ge, gs, gn: (l, ge[g], 0, fsel(f, gn, g))),
                pl.BlockSpec((None, None, D, tf), lambda g, f, ge, gs, gn: (l, ge[g], 0, fsel(f, gn, g))),
                pl.BlockSpec((None, None, tf, D), lambda g, f, ge, gs, gn: (l, ge[g], fsel(f, gn, g), 0)),
                pl.BlockSpec((None, None, 1, tf), lambda g, f, ge, gs, gn: (l, ge[g], 0, fsel(f, gn, g))),
                pl.BlockSpec((None, None, 1, tf), lambda g, f, ge, gs, gn: (l, ge[g], 0, fsel(f, gn, g))),
                pl.BlockSpec((None, None, 1, D), lambda g, f, ge, gs, gn: (l, ge[g], 0, 0)),
            ],
            out_specs=pl.BlockSpec(memory_space=pl.ANY),
            scratch_shapes=[
                pltpu.VMEM((GROUP_CHUNKS, EXPERT_CHUNK, D), BF16),
                pltpu.VMEM((GROUP_CHUNKS, EXPERT_CHUNK, D), F32),
                pltpu.VMEM((D, tf), BF16),
                pltpu.VMEM((D, tf), BF16),
                pltpu.VMEM((tf, D), BF16),
                pltpu.SemaphoreType.DMA((GROUP_CHUNKS,)),
                pltpu.SemaphoreType.DMA((GROUP_CHUNKS,)),
            ],
        ),
        out_shape=jax.ShapeDtypeStruct((rows, D), F32),
        compiler_params=_cp(("arbitrary", "arbitrary")),
        name="grouped_experts",
    )(ge, gs, gn, xs, w_gate, w_up, w_down, b_gate, b_up, b_down)


def _combine_kernel(dest_ref, os_hbm, x1_ref, g2_ref, route_ref, o_ref, buf, sem, *, tm):
    i = pl.program_id(0)
    nt = pl.num_programs(0)

    def issue(tile, slot):
        def body(r, c):
            t = tile * tm + r
            for k in range(TOP_K):
                pltpu.make_async_copy(os_hbm.at[pl.ds(dest_ref[t * TOP_K + k], 1)],
                                      buf.at[slot, k, pl.ds(r, 1)], sem.at[slot]).start()
            return c

        lax.fori_loop(0, tm, body, 0)

    @pl.when(i == 0)
    def _():
        issue(0, 0)

    @pl.when(i + 1 < nt)
    def _():
        issue(i + 1, (i + 1) % 2)

    slot = i % 2
    for k in range(TOP_K):
        pltpu.make_async_copy(os_hbm.at[pl.ds(0, tm)], buf.at[slot, k], sem.at[slot]).wait()
    route = route_ref[...]
    y = route[:, 2 * TOP_K:2 * TOP_K + 1] * buf[slot, 0]
    for k in range(1, TOP_K):
        y = y + route[:, 2 * TOP_K + k:2 * TOP_K + k + 1] * buf[slot, k]
    o_ref[...] = x1_ref[...] + g2_ref[...] * y


def _combine_call(dest, os_, x1, modt, route, tm):
    T, D = x1.shape
    return pl.pallas_call(
        functools.partial(_combine_kernel, tm=tm),
        grid_spec=pltpu.PrefetchScalarGridSpec(
            num_scalar_prefetch=1,
            grid=(T // tm,),
            in_specs=[
                pl.BlockSpec(memory_space=pl.ANY),
                pl.BlockSpec((tm, D), lambda i, d: (i, 0)),
                pl.BlockSpec((None, 1, D), lambda i, d: (i, 0, 5)),
                pl.BlockSpec((tm, LANES), lambda i, d: (i, 0)),
            ],
            out_specs=pl.BlockSpec((tm, D), lambda i, d: (i, 0)),
            scratch_shapes=[pltpu.VMEM((2, TOP_K, tm, D), F32), pltpu.SemaphoreType.DMA((2,))],
        ),
        out_shape=jax.ShapeDtypeStruct((T, D), F32),
        compiler_params=_cp(("arbitrary",)),
        name="expert_combine",
    )(dest, os_, x1, modt, route)


def _rope_tables(pos_a, pos_b, half, reps):
    freqs = ROPE_BASE ** (-jnp.arange(half, dtype=F32) / half)

    def blk(pos):
        ang = pos.astype(F32)[:, None] * freqs[None, :]
        c, s = jnp.cos(ang), jnp.sin(ang)
        return jnp.concatenate([c, c], -1), jnp.concatenate([-s, s], -1)

    ca, sa = blk(pos_a)
    cb, sb = blk(pos_b)
    cos = jnp.tile(jnp.concatenate([ca, cb], -1), (1, reps))
    sin = jnp.tile(jnp.concatenate([sa, sb], -1), (1, reps))
    return cos, sin


def _cond_tiles(T, Tp, Ss, tm):
    starts = np.arange(0, T, tm)
    return np.where(starts < Tp, 0, 1 + np.maximum(starts - Tp, 0) // Ss).astype(np.int32)


def _pick_tile(limit, *sizes):
    t = ROW_ALIGN
    while t * 2 <= limit and all(s % (t * 2) == 0 for s in sizes):
        t *= 2
    return t


def _route_plan(route, cnt, E, G):
    idx = route[:, 0:TOP_K].astype(jnp.int32)
    rank = route[:, TOP_K:2 * TOP_K].astype(jnp.int32)
    counts = cnt[0, :E].astype(jnp.int32)
    nch = (counts + EXPERT_CHUNK - 1) // EXPERT_CHUNK
    padded = nch * EXPERT_CHUNK
    base = jnp.cumsum(padded) - padded
    dest = (base[idx] + rank).reshape(-1)
    ngr = (nch + GROUP_CHUNKS - 1) // GROUP_CHUNKS
    gend = jnp.cumsum(ngr)
    gid = jnp.arange(G, dtype=jnp.int32)
    valid = gid < gend[-1]
    ge = jnp.minimum(jnp.sum((gid[:, None] >= gend[None, :]).astype(jnp.int32), axis=1), E - 1)
    j = gid - (gend - ngr)[ge]
    gs = base[ge] + j * (GROUP_CHUNKS * EXPERT_CHUNK)
    gn = jnp.clip(nch[ge] - j * GROUP_CHUNKS, 0, GROUP_CHUNKS)
    last = jnp.maximum(gend[-1] - 1, 0)
    ge = jnp.where(valid, ge, ge[last])
    gs = jnp.where(valid, gs, 0)
    gn = jnp.where(valid, gn, 0)
    return (dest.astype(jnp.int32), (base + counts).astype(jnp.int32), (padded - counts).astype(jnp.int32),
            ge.astype(jnp.int32), gs.astype(jnp.int32), gn.astype(jnp.int32))


def kernel(x_prompt, x_sample, cache_diff_k, cache_diff_v, cache_mla_ckv, cache_mla_krope, state_ret_fwd, state_ret_bwd, c, c_ctx, w_mod, b_mod, norm1, norm2, w_in, diff_q_gain, diff_k_gain, diff_lambda, diff_subln, mla_q_gain, w_uq, mla_kv_gain, w_ukv, mla_qn_gain, mla_kn_gain, ret_decay, w_br_diff, w_br_mla, w_br_ret, w_o, w_router, b_router, w_gate, b_gate, w_up, b_up, w_down, b_down):
    Bp, Sp, D = x_prompt.shape
    Bs, Ss, _ = x_sample.shape
    L = w_mod.shape[0]
    P = cache_diff_k.shape[2]
    E = w_router.shape[-1]
    F = w_gate.shape[-1]
    Tp, Ts = Bp * Sp, Bs * Ss
    T = Tp + Ts
    assert 1 + Bs <= COND_ROWS and E <= LANES
    assert Tp % ROW_ALIGN == 0 and Ss % ROW_ALIGN == 0 and Tp % Ss == 0 and Sp % RET_CHUNK == 0
    assert (3 * D) % 512 == 0 and F % 512 == 0

    off = 3 * D
    o_gl = sum((512, 512, 512, 512, 256, 64, 256, 256, 512, 512, 512))
    w_in_p = jnp.concatenate([
        w_in[..., o_gl:], w_in[..., 0:2304], w_in[..., 2368:2880], w_in[..., 2304:2368],
        jnp.zeros((L, D, 192), w_in.dtype), w_in[..., 2880:o_gl]], axis=-1).astype(BF16)
    assert w_in_p.shape[-1] == off + _SEC_TOTAL
    cond = jnp.zeros((COND_ROWS, D), F32).at[0].set(c_ctx).at[1:1 + Bs].set(c)
    wuq_p = jnp.pad(w_uq, ((0, 0), (0, 0), (0, 0), (0, MLA_NOPE + MLA_VD - MLA_QK))).reshape(
        L, MLA_Q_RANK, -1).astype(BF16)
    wukv_p = w_ukv.reshape(L, MLA_KV_RANK, -1).astype(BF16)
    pad_qk = ((0, 0), (0, MLA_NOPE + MLA_VD - MLA_QK))
    mla_w = (wuq_p, wukv_p, mla_q_gain[:, None, :], mla_kv_gain[:, None, :],
             jnp.pad(mla_qn_gain, pad_qk)[:, None, :], jnp.pad(mla_kn_gain, pad_qk)[:, None, :])
    diff_g = (jnp.tile(diff_q_gain, (1, 2))[:, None, :], jnp.tile(diff_k_gain, (1, 2))[:, None, :],
              diff_subln[:, None, :])
    wd_b, wm_b, wr_b, wo_b = (w.astype(BF16) for w in (w_br_diff, w_br_mla, w_br_ret, w_o))
    w_router_p = jnp.pad(w_router, ((0, 0), (0, 0), (0, LANES - E)))
    b_router_p = jnp.pad(b_router, ((0, 0), (0, LANES - E)))[:, None, :]
    lam_vec = diff_lambda.astype(F32)
    lam_init = jnp.asarray([0.8 - 0.6 * math.exp(-0.3 * l) for l in range(L)], F32)
    lam = (jnp.exp(jnp.sum(lam_vec[:, 0] * lam_vec[:, 1], -1))
           - jnp.exp(jnp.sum(lam_vec[:, 2] * lam_vec[:, 3], -1)) + lam_init)
    log_g = jax.nn.log_sigmoid(ret_decay.astype(F32)).reshape(-1)
    cdk = cache_diff_k.reshape(Bs, L, P, DIFF_HEADS * 2 * DIFF_HD)
    cdv = cache_diff_v.reshape(Bs, L, P, DIFF_HEADS * DIFF_VD)
    ckr = jnp.pad(cache_mla_krope, ((0, 0), (0, 0), (0, 0), (0, LANES - MLA_ROPE)))
    t_pos = jnp.arange(Ss, dtype=jnp.int32)
    tabs_ax = _rope_tables(t_pos // GRID_W, t_pos % GRID_W, 16, 2)
    tabs_1d = _rope_tables(t_pos, t_pos, 32, 1)
    bg4, bu4, bd4 = b_gate[:, :, None, :], b_up[:, :, None, :], b_down[:, :, None, :]

    tm_p = _pick_tile(1024, Tp, Ss)
    tm_r = ROW_ALIGN
    tq = ROW_ALIGN
    rows_sorted = T * TOP_K + E * EXPERT_CHUNK
    G = E + (T * TOP_K) // (GROUP_CHUNKS * EXPERT_CHUNK)
    tf = 512

    mod = _mod_call(cond, w_mod, b_mod)
    x = jnp.concatenate([x_prompt.reshape(Tp, D), x_sample.reshape(Ts, D)], axis=0)
    outs = [[] for _ in range(6)]
    for l in range(L):
        modt_p = mod[l][_cond_tiles(T, Tp, Ss, tm_p)][:, None, :]
        modt_r = mod[l][_cond_tiles(T, Tp, Ss, tm_r)][:, None, :]
        proj = _proj_call(l, x, modt_p, norm1[:, None, :], w_in_p, tm_p, 512)

        yd_p, kn_p = _diff_call(l, proj, off, lam, diff_g, Bp, Sp, 0, min(tq, Sp), False)
        (yd_s,) = _diff_call(l, proj, off, lam, diff_g, Bs, Ss, Tp, tq, True, (cdk, cdv), tabs_ax)
        ym_p, ckv_p = _mla_call(l, proj, off, mla_w, Bp, Sp, 0, min(tq, Sp), False)
        (ym_s,) = _mla_call(l, proj, off, mla_w, Bs, Ss, Tp, tq, True, (cache_mla_ckv, ckr), tabs_ax)
        yr_p, sf_p, sb_p = _ret_call(l, proj, off, log_g, Bp, Sp, 0, False)
        (yr_s,) = _ret_call(l, proj, off, log_g, Bs, Ss, Tp, True, (state_ret_fwd, state_ret_bwd), tabs_1d)

        yd = jnp.concatenate([yd_p, yd_s], axis=0)
        ym = jnp.concatenate([ym_p, ym_s], axis=0)
        yr = jnp.concatenate([yr_p, yr_s], axis=0)
        merged = _merge_call(l, yd, ym, yr, proj, wd_b, wm_b, wr_b, tm_r)
        x1, h2, route, cnt = _post_call(l, merged, x, wo_b, modt_r, norm2[:, None, :],
                                        w_router_p, b_router_p, E, tm_r)
        dest, pstart, pcnt, ge, gs, gn = _route_plan(route, cnt, E, G)
        xs = _dispatch_call(h2, dest, pstart, pcnt, rows_sorted, E, tm_r)
        os_ = _expert_call(l, xs, ge, gs, gn, w_gate, w_up, w_down, bg4, bu4, bd4, tf)
        x = _combine_call(dest, os_, x1, modt_r, route, tm_r)

        dv = proj[:Tp, off + _SEC["DV"]:off + _SEC["DV"] + DIFF_HEADS * DIFF_VD]
        kr = proj[:Tp, off + _SEC["MKR"]:off + _SEC["MKR"] + MLA_ROPE]
        for lst, v in zip(outs, (kn_p.reshape(Bp, Sp, DIFF_HEADS, 2 * DIFF_HD),
                                 dv.reshape(Bp, Sp, DIFF_HEADS, DIFF_VD),
                                 ckv_p.reshape(Bp, Sp, MLA_KV_RANK), kr.reshape(Bp, Sp, MLA_ROPE),
                                 sf_p, sb_p)):
            lst.append(v)

    y_prompt = x[:Tp].reshape(Bp, Sp, D)
    y_sample = x[Tp:].reshape(Bs, Ss, D)
    return (y_prompt, y_sample) + tuple(jnp.stack(o, axis=1) for o in outs)
```

```python
import functools
import math

import jax
import jax.numpy as jnp
import numpy as np
from jax import lax
from jax.experimental import pallas as pl
from jax.experimental.pallas import tpu as pltpu

F32 = jnp.float32
BF16 = jnp.bfloat16

GRID_W = 64
DIFF_HEADS = 4
DIFF_HD = 64
DIFF_VD = 128
MLA_HEADS = 8
MLA_Q_RANK = 512
MLA_KV_RANK = 256
MLA_NOPE = 128
MLA_ROPE = 64
MLA_VD = 128
MLA_QK = MLA_NOPE + MLA_ROPE
RET_HEADS = 4
RET_QK = 64
RET_VD = 128
RET_CHUNK = 128
TOP_K = 4
SWIGLU_LIMIT = 7.0
SWIGLU_ALPHA = 1.702
ROPE_BASE = 10000.0
EPS = 1e-6

LANES = 128
ROW_ALIGN = 256
EXPERT_CHUNK = 256
GROUP_CHUNKS = 4
COND_ROWS = 8
VMEM_LIMIT = 56 * 1024 * 1024

_SEC = dict(DQ=0, DK=512, DV=1024, MQ=1536, MKV=2048, RQ=2304, RK=2560, MKR=2816,
            RV=3072, RGF=3584, RGB=4096)
_SEC_TOTAL = 4608


def _cp(sem, vmem=VMEM_LIMIT):
    return pltpu.CompilerParams(dimension_semantics=sem, vmem_limit_bytes=vmem)


def _lane_iota(shape):
    return lax.broadcasted_iota(jnp.int32, shape, len(shape) - 1)


def _rms(x, n=None):
    n = x.shape[-1] if n is None else n
    return x * lax.rsqrt(jnp.sum(x * x, axis=-1, keepdims=True) / n + EPS)


def _dot(a, b):
    return jnp.dot(a, b, preferred_element_type=F32)


def _dot_nt(a, b):
    return lax.dot_general(a, b, (((1,), (1,)), ((), ())), preferred_element_type=F32)


def _softmax(s):
    e = jnp.exp(s - jnp.max(s, axis=-1, keepdims=True))
    return e / jnp.sum(e, axis=-1, keepdims=True)


def _sigmoid(x):
    return 1.0 / (1.0 + jnp.exp(-x))


def _rope(x, cos, sin, half):
    lane = _lane_iota(x.shape)
    up = pltpu.roll(x, LANES - half, 1)
    dn = pltpu.roll(x, half, 1)
    sw = jnp.where(lane % (2 * half) < half, up, dn)
    return x * cos + sw * sin


def _norm_halves(x, gain):
    lane = _lane_iota(x.shape)
    lo = lane < DIFF_HD
    x2 = x * x
    ss_lo = jnp.sum(jnp.where(lo, x2, 0.0), axis=-1, keepdims=True)
    ss_hi = jnp.sum(jnp.where(lo, 0.0, x2), axis=-1, keepdims=True)
    inv = jnp.where(lo, lax.rsqrt(ss_lo / DIFF_HD + EPS), lax.rsqrt(ss_hi / DIFF_HD + EPS))
    return x * inv * gain


def _mod_kernel(c_ref, w_ref, b_ref, o_ref):
    c = c_ref[...]
    a = (c * _sigmoid(c)).astype(BF16)
    o_ref[...] = _dot(a, w_ref[...].astype(BF16)) + b_ref[...]


def _mod_call(cond, w_mod, b_mod):
    L, D, N = w_mod.shape
    tn = 1024 if N % 1024 == 0 else 512
    return pl.pallas_call(
        _mod_kernel,
        grid=(L, N // tn),
        in_specs=[
            pl.BlockSpec((COND_ROWS, D), lambda l, j: (0, 0)),
            pl.BlockSpec((None, D, tn), lambda l, j: (l, 0, j)),
            pl.BlockSpec((None, 1, tn), lambda l, j: (l, 0, j)),
        ],
        out_specs=pl.BlockSpec((None, COND_ROWS, tn), lambda l, j: (l, 0, j)),
        out_shape=jax.ShapeDtypeStruct((L, COND_ROWS, N), F32),
        compiler_params=_cp(("arbitrary", "arbitrary")),
        name="adaln_mod",
    )(cond, w_mod, b_mod.reshape(L, 1, N))


def _proj_kernel(x_ref, sc_ref, sh_ref, g_ref, w_ref, o_ref, h_ref):
    @pl.when(pl.program_id(1) == 0)
    def _():
        h = _rms(x_ref[...]) * g_ref[...]
        h_ref[...] = (h * (1.0 + sc_ref[...]) + sh_ref[...]).astype(BF16)

    o_ref[...] = _dot(h_ref[...], w_ref[...])


def _proj_call(l, x, modt, norm1, w_in_p, tm, tn):
    T, D = x.shape
    PC = w_in_p.shape[-1]
    return pl.pallas_call(
        _proj_kernel,
        grid=(T // tm, PC // tn),
        in_specs=[
            pl.BlockSpec((tm, D), lambda i, j: (i, 0)),
            pl.BlockSpec((None, 1, D), lambda i, j: (i, 0, 1)),
            pl.BlockSpec((None, 1, D), lambda i, j: (i, 0, 0)),
            pl.BlockSpec((None, 1, D), lambda i, j: (l, 0, 0)),
            pl.BlockSpec((None, D, tn), lambda i, j: (l, 0, j)),
        ],
        out_specs=pl.BlockSpec((tm, tn), lambda i, j: (i, j)),
        out_shape=jax.ShapeDtypeStruct((T, PC), F32),
        scratch_shapes=[pltpu.VMEM((tm, D), BF16)],
        compiler_params=_cp(("arbitrary", "arbitrary")),
        name="norm_proj",
    )(x, modt, modt, norm1, w_in_p)


def _diff_kernel(*refs, latent, S, P, lam_init, l):
    if latent:
        (lam_ref, q_ref, k_ref, v_ref, qg_ref, kg_ref, sg_ref, ck_ref, cv_ref,
         cosq_ref, sinq_ref, cosk_ref, sink_ref, y_ref, kall, vall) = refs
    else:
        (lam_ref, q_ref, k_ref, v_ref, qg_ref, kg_ref, sg_ref, y_ref, kn_ref, kall, vall) = refs

    @pl.when(pl.program_id(1) == 0)
    def _():
        for h in range(DIFF_HEADS):
            sl = slice(LANES * h, LANES * (h + 1))
            kn = _norm_halves(k_ref[:, sl], kg_ref[...])
            if latent:
                kn = _rope(kn, cosk_ref[...], sink_ref[...], 16)
            else:
                kn_ref[:, sl] = kn
            kall[0:S, sl] = kn.astype(BF16)
        vall[0:S, :] = v_ref[...].astype(BF16)
        if latent:
            kall[S:S + P, :] = ck_ref[...].astype(BF16)
            vall[S:S + P, :] = cv_ref[...].astype(BF16)

    lam = lam_ref[l]
    scale = DIFF_HD ** -0.5
    for h in range(DIFF_HEADS):
        sl = slice(LANES * h, LANES * (h + 1))
        qn = _norm_halves(q_ref[:, sl], qg_ref[...])
        if latent:
            qn = _rope(qn, cosq_ref[...], sinq_ref[...], 16)
        lo = _lane_iota(qn.shape) < DIFF_HD
        q1 = jnp.where(lo, qn, 0.0).astype(BF16)
        q2 = jnp.where(lo, 0.0, qn).astype(BF16)
        kh = kall[:, sl]
        p = _softmax(_dot_nt(q1, kh) * scale) - lam * _softmax(_dot_nt(q2, kh) * scale)
        o = _dot(p.astype(BF16), vall[:, sl])
        y_ref[:, sl] = (_rms(o) * sg_ref[...] * (1.0 - lam_init)).astype(BF16)


def _diff_call(l, proj, off, lam, gains, B, S, row0, tq, latent, cache=None, tabs=None):
    qg, kg, sg = gains
    W = DIFF_HEADS * LANES
    nq = S // tq
    P = cache[0].shape[2] if latent else 0
    lam_init = 0.8 - 0.6 * math.exp(-0.3 * l)
    cq, ck, cv = ((off + _SEC[n]) // W for n in ("DQ", "DK", "DV"))
    in_specs = [
        pl.BlockSpec(memory_space=pltpu.SMEM),
        pl.BlockSpec((tq, W), lambda b, i: (row0 // tq + b * nq + i, cq)),
        pl.BlockSpec((S, W), lambda b, i: (row0 // S + b, ck)),
        pl.BlockSpec((S, W), lambda b, i: (row0 // S + b, cv)),
        pl.BlockSpec((None, 1, LANES), lambda b, i: (l, 0, 0)),
        pl.BlockSpec((None, 1, LANES), lambda b, i: (l, 0, 0)),
        pl.BlockSpec((None, 1, LANES), lambda b, i: (l, 0, 0)),
    ]
    args = [lam, proj, proj, proj, qg, kg, sg]
    out_shape = [jax.ShapeDtypeStruct((B * S, W), BF16)]
    out_specs = [pl.BlockSpec((tq, W), lambda b, i: (b * nq + i, 0))]
    if latent:
        cos, sin = tabs
        in_specs += [
            pl.BlockSpec((None, None, P, W), lambda b, i: (b, l, 0, 0)),
            pl.BlockSpec((None, None, P, W), lambda b, i: (b, l, 0, 0)),
            pl.BlockSpec((tq, LANES), lambda b, i: (i, 0)),
            pl.BlockSpec((tq, LANES), lambda b, i: (i, 0)),
            pl.BlockSpec((S, LANES), lambda b, i: (0, 0)),
            pl.BlockSpec((S, LANES), lambda b, i: (0, 0)),
        ]
        args += [cache[0], cache[1], cos, sin, cos, sin]
    else:
        out_shape.append(jax.ShapeDtypeStruct((B * S, W), F32))
        out_specs.append(pl.BlockSpec((S, W), lambda b, i: (b, 0)))
    return pl.pallas_call(
        functools.partial(_diff_kernel, latent=latent, S=S, P=P, lam_init=lam_init, l=l),
        grid=(B, nq),
        in_specs=in_specs,
        out_specs=out_specs,
        out_shape=out_shape,
        scratch_shapes=[pltpu.VMEM((S + P, W), BF16), pltpu.VMEM((S + P, W), BF16)],
        compiler_params=_cp(("arbitrary", "arbitrary")),
        name="diff_attn_lat" if latent else "diff_attn_ctx",
    )(*args)


def _mla_kernel(*refs, latent, S, P):
    if latent:
        (mq_ref, mkv_ref, mkr_ref, wuq_ref, wukv_ref, qg_ref, kvg_ref, qng_ref, kng_ref,
         cckv_ref, ckr_ref, cosq_ref, sinq_ref, cosk_ref, sink_ref, y_ref, kall, vall) = refs
    else:
        (mq_ref, mkv_ref, mkr_ref, wuq_ref, wukv_ref, qg_ref, kvg_ref, qng_ref, kng_ref,
         y_ref, ckv_ref, kall, vall) = refs
    HW = MLA_NOPE + MLA_VD

    @pl.when(pl.program_id(1) == 0)
    def _():
        ckv = _rms(mkv_ref[...]) * kvg_ref[...]
        if not latent:
            ckv_ref[...] = ckv
        segs = [(0, S, ckv, mkr_ref[...], latent)]
        if latent:
            segs.append((S, P, cckv_ref[...], ckr_ref[...], False))
        for r0, R, cv, kr, rot in segs:
            kv = _dot(cv.astype(BF16), wukv_ref[...])
            kr_ss = jnp.sum(kr * kr, axis=-1, keepdims=True)
            krg = kr * kng_ref[:, MLA_NOPE:]
            if rot:
                krg = _rope(krg, cosk_ref[...], sink_ref[...], 16)
            for h in range(MLA_HEADS):
                kn = kv[:, HW * h:HW * h + MLA_NOPE]
                ss = jnp.sum(kn * kn, axis=-1, keepdims=True) + kr_ss
                r = lax.rsqrt(ss / MLA_QK + EPS)
                kall[h, r0:r0 + R, 0:MLA_NOPE] = (kn * r * kng_ref[:, 0:MLA_NOPE]).astype(BF16)
                kall[h, r0:r0 + R, MLA_NOPE:] = (krg * r).astype(BF16)
                vall[h, r0:r0 + R, :] = kv[:, HW * h + MLA_NOPE:HW * (h + 1)].astype(BF16)

    cq = (_rms(mq_ref[...]) * qg_ref[...]).astype(BF16)
    qm = _dot(cq, wuq_ref[...])
    scale = MLA_QK ** -0.5
    for h in range(MLA_HEADS):
        qn = _rms(qm[:, HW * h:HW * (h + 1)], MLA_QK) * qng_ref[...]
        if latent:
            qr = _rope(qn[:, MLA_NOPE:], cosq_ref[...], sinq_ref[...], 16)
            qn = jnp.concatenate([qn[:, 0:MLA_NOPE], qr], axis=-1)
        p = _softmax(_dot_nt(qn.astype(BF16), kall[h]) * scale)
        y_ref[:, MLA_VD * h:MLA_VD * (h + 1)] = _dot(p.astype(BF16), vall[h]).astype(BF16)


def _mla_call(l, proj, off, weights, B, S, row0, tq, latent, cache=None, tabs=None):
    wuq, wukv, qg, kvg, qng, kng = weights
    nq = S // tq
    P = cache[0].shape[2] if latent else 0
    HW = MLA_NOPE + MLA_VD
    NW = MLA_HEADS * HW
    c_mq = (off + _SEC["MQ"]) // MLA_Q_RANK
    c_mkv = (off + _SEC["MKV"]) // MLA_KV_RANK
    c_mkr = (off + _SEC["MKR"]) // LANES

    def lay(shape):
        return pl.BlockSpec((None,) + shape, lambda b, i: (l,) + (0,) * len(shape))

    in_specs = [
        pl.BlockSpec((tq, MLA_Q_RANK), lambda b, i: (row0 // tq + b * nq + i, c_mq)),
        pl.BlockSpec((S, MLA_KV_RANK), lambda b, i: (row0 // S + b, c_mkv)),
        pl.BlockSpec((S, LANES), lambda b, i: (row0 // S + b, c_mkr)),
        lay((MLA_Q_RANK, NW)), lay((MLA_KV_RANK, NW)),
        lay((1, MLA_Q_RANK)), lay((1, MLA_KV_RANK)), lay((1, HW)), lay((1, HW)),
    ]
    args = [proj, proj, proj, wuq, wukv, qg, kvg, qng, kng]
    out_shape = [jax.ShapeDtypeStruct((B * S, MLA_HEADS * MLA_VD), BF16)]
    out_specs = [pl.BlockSpec((tq, MLA_HEADS * MLA_VD), lambda b, i: (b * nq + i, 0))]
    if latent:
        cos, sin = tabs
        in_specs += [
            pl.BlockSpec((None, None, P, MLA_KV_RANK), lambda b, i: (b, l, 0, 0)),
            pl.BlockSpec((None, None, P, LANES), lambda b, i: (b, l, 0, 0)),
            pl.BlockSpec((tq, LANES), lambda b, i: (i, 0)),
            pl.BlockSpec((tq, LANES), lambda b, i: (i, 0)),
            pl.BlockSpec((S, LANES), lambda b, i: (0, 0)),
            pl.BlockSpec((S, LANES), lambda b, i: (0, 0)),
        ]
        args += [cache[0], cache[1], cos, sin, cos, sin]
    else:
        out_shape.append(jax.ShapeDtypeStruct((B * S, MLA_KV_RANK), F32))
        out_specs.append(pl.BlockSpec((S, MLA_KV_RANK), lambda b, i: (b, 0)))
    return pl.pallas_call(
        functools.partial(_mla_kernel, latent=latent, S=S, P=P),
        grid=(B, nq),
        in_specs=in_specs,
        out_specs=out_specs,
        out_shape=out_shape,
        scratch_shapes=[pltpu.VMEM((MLA_HEADS, S + P, HW), BF16),
                        pltpu.VMEM((MLA_HEADS, S + P, MLA_VD), BF16)],
        compiler_params=_cp(("arbitrary", "arbitrary")),
        name="mla_lat" if latent else "mla_ctx",
    )(*args)


def _ret_kernel(*refs, latent, S, l):
    if latent:
        (lg_ref, rq_ref, rk_ref, rv_ref, gf_ref, gb_ref, cos_ref, sin_ref, s0f_ref, s0b_ref,
         y_ref, qs, ks, of, ob) = refs
    else:
        (lg_ref, rq_ref, rk_ref, rv_ref, gf_ref, gb_ref, y_ref, sf_ref, sb_ref, qs, ks, of, ob) = refs
    C = RET_CHUNK
    nc = S // C
    ii = lax.broadcasted_iota(jnp.int32, (C, C), 0).astype(F32)
    jj = lax.broadcasted_iota(jnp.int32, (C, C), 1).astype(F32)
    col = lax.broadcasted_iota(jnp.int32, (C, 1), 0).astype(F32)
    one = jnp.ones((1, 1), F32)
    for s in range(RET_HEADS // 2):
        ssl = slice(LANES * s, LANES * (s + 1))
        qslab = rq_ref[:, ssl]
        kslab = rk_ref[:, ssl] * (RET_QK ** -0.5)
        if latent:
            qslab = _rope(qslab, cos_ref[...], sin_ref[...], 32)
            kslab = _rope(kslab, cos_ref[...], sin_ref[...], 32)
        lo = _lane_iota(qslab.shape) < RET_QK
        for u in range(2):
            h = 2 * s + u
            vsl = slice(RET_VD * h, RET_VD * (h + 1))
            keep = lo if u == 0 else jnp.logical_not(lo)
            qs[...] = jnp.where(keep, qslab, 0.0)
            ks[...] = jnp.where(keep, kslab, 0.0)
            for d in range(2):
                lg = lg_ref[(l * 2 + d) * RET_HEADS + h]
                if d == 0:
                    rel = ii - jj
                    qdec = jnp.exp((col + 1.0) * lg)
                    kdec = jnp.exp((C - 1.0 - col) * lg)
                else:
                    rel = jj - ii
                    qdec = jnp.exp((C - col) * lg)
                    kdec = jnp.exp(col * lg)
                dmat = jnp.where(rel >= 0, jnp.exp(jnp.maximum(rel, 0.0) * lg), 0.0)
                cdec = jnp.exp(one * (C * lg))
                o_ref = of if d == 0 else ob
                if latent:
                    s0 = (s0f_ref if d == 0 else s0b_ref)[h]
                    z = jnp.zeros_like(s0)
                    st0 = jnp.concatenate([s0, z] if u == 0 else [z, s0], axis=0)
                else:
                    st0 = jnp.zeros((LANES, RET_VD), F32)

                def body(n, st, d=d, dmat=dmat, qdec=qdec, kdec=kdec, cdec=cdec, o_ref=o_ref, vsl=vsl):
                    cidx = n if d == 0 else nc - 1 - n
                    r0 = pl.multiple_of(cidx * C, C)
                    qc = qs[pl.ds(r0, C), :]
                    kc = ks[pl.ds(r0, C), :]
                    vc = rv_ref[pl.ds(r0, C), vsl].astype(BF16)
                    sc = _dot_nt(qc.astype(BF16), kc.astype(BF16)) * dmat
                    o = _dot(sc.astype(BF16), vc) + _dot((qc * qdec).astype(BF16), st.astype(BF16))
                    o_ref[pl.ds(r0, C), :] = o
                    kt = jnp.transpose(kc * kdec).astype(BF16)
                    return st * cdec + _dot(kt, vc)

                st = lax.fori_loop(0, nc, body, st0)
                if not latent:
                    (sf_ref if d == 0 else sb_ref)[h] = st[RET_QK * u:RET_QK * (u + 1), :]
            gf = gf_ref[:, vsl]
            gb = gb_ref[:, vsl]
            y = _rms(of[...]) * (gf * _sigmoid(gf)) + _rms(ob[...]) * (gb * _sigmoid(gb))
            y_ref[:, vsl] = y.astype(BF16)


def _ret_call(l, proj, off, lg, B, S, row0, latent, states=None, tabs=None):
    QW = RET_HEADS * RET_QK
    VW = RET_HEADS * RET_VD
    c_rq, c_rk = (off + _SEC["RQ"]) // QW, (off + _SEC["RK"]) // QW
    c_rv, c_gf, c_gb = ((off + _SEC[n]) // VW for n in ("RV", "RGF", "RGB"))
    rb = row0 // S
    in_specs = [
        pl.BlockSpec(memory_space=pltpu.SMEM),
        pl.BlockSpec((S, QW), lambda b: (rb + b, c_rq)),
        pl.BlockSpec((S, QW), lambda b: (rb + b, c_rk)),
        pl.BlockSpec((S, VW), lambda b: (rb + b, c_rv)),
        pl.BlockSpec((S, VW), lambda b: (rb + b, c_gf)),
        pl.BlockSpec((S, VW), lambda b: (rb + b, c_gb)),
    ]
    args = [lg, proj, proj, proj, proj, proj]
    out_shape = [jax.ShapeDtypeStruct((B * S, VW), BF16)]
    out_specs = [pl.BlockSpec((S, VW), lambda b: (b, 0))]
    st_block = (None, RET_HEADS, RET_QK, RET_VD)
    if latent:
        cos, sin = tabs
        in_specs += [
            pl.BlockSpec((S, LANES), lambda b: (0, 0)),
            pl.BlockSpec((S, LANES), lambda b: (0, 0)),
            pl.BlockSpec((None,) + st_block, lambda b: (b, l, 0, 0, 0)),
            pl.BlockSpec((None,) + st_block, lambda b: (b, l, 0, 0, 0)),
        ]
        args += [cos, sin, states[0], states[1]]
    else:
        for _ in range(2):
            out_shape.append(jax.ShapeDtypeStruct((B, RET_HEADS, RET_QK, RET_VD), F32))
            out_specs.append(pl.BlockSpec(st_block, lambda b: (b, 0, 0, 0)))
    return pl.pallas_call(
        functools.partial(_ret_kernel, latent=latent, S=S, l=l),
        grid=(B,),
        in_specs=in_specs,
        out_specs=out_specs,
        out_shape=out_shape,
        scratch_shapes=[pltpu.VMEM((S, LANES), F32), pltpu.VMEM((S, LANES), F32),
                        pltpu.VMEM((S, RET_VD), F32), pltpu.VMEM((S, RET_VD), F32)],
        compiler_params=_cp(("arbitrary",)),
        name="retention_lat" if latent else "retention_ctx",
    )(*args)


def _merge_kernel(yd_ref, ym_ref, yr_ref, g0_ref, g1_ref, g2_ref, wd_ref, wm_ref, wr_ref, o_ref):
    acc = _sigmoid(g0_ref[...]) * _dot(yd_ref[...], wd_ref[...])
    acc = acc + _sigmoid(g1_ref[...]) * _dot(ym_ref[...], wm_ref[...])
    acc = acc + _sigmoid(g2_ref[...]) * _dot(yr_ref[...], wr_ref[...])
    o_ref[...] = acc.astype(BF16)


def _merge_call(l, yd, ym, yr, proj, wd, wm, wr, tm):
    T = yd.shape[0]
    D = wd.shape[-1]

    def rows(w):
        return pl.BlockSpec((tm, w), lambda i: (i, 0))

    def gate(k):
        return pl.BlockSpec((tm, D), lambda i: (i, k))

    def wt(w):
        return pl.BlockSpec((None, w.shape[1], D), lambda i: (l, 0, 0))

    return pl.pallas_call(
        _merge_kernel,
        grid=(T // tm,),
        in_specs=[rows(yd.shape[1]), rows(ym.shape[1]), rows(yr.shape[1]),
                  gate(0), gate(1), gate(2), wt(wd), wt(wm), wt(wr)],
        out_specs=pl.BlockSpec((tm, D), lambda i: (i, 0)),
        out_shape=jax.ShapeDtypeStruct((T, D), BF16),
        compiler_params=_cp(("arbitrary",)),
        name="branch_merge",
    )(yd, ym, yr, proj, proj, proj, wd, wm, wr)


def _post_kernel(m_ref, x_ref, wo_ref, g1_ref, sc_ref, sh_ref, n2_ref, wr_ref, br_ref,
                 x1_ref, h2_ref, route_ref, cnt_ref, run_ref, *, E, tm):
    @pl.when(pl.program_id(0) == 0)
    def _():
        run_ref[...] = jnp.zeros_like(run_ref)

    x1 = x_ref[...] + g1_ref[...] * _dot(m_ref[...], wo_ref[...])
    x1_ref[...] = x1
    h = _rms(x1) * n2_ref[...]
    h = h * (1.0 + sc_ref[...]) + sh_ref[...]
    h2_ref[...] = h
    logits = jnp.dot(h, wr_ref[...], preferred_element_type=F32,
                     precision=lax.Precision.HIGHEST) + br_ref[...]
    lane = _lane_iota(logits.shape)
    cur = jnp.where(lane < E, logits, -jnp.inf)
    hots, vals = [], []
    for _ in range(TOP_K):
        m = jnp.max(cur, axis=-1, keepdims=True)
        idx = jnp.min(jnp.where(cur == m, lane, LANES), axis=-1, keepdims=True)
        hot = lane == idx
        hots.append(hot)
        vals.append(m)
        cur = jnp.where(hot, -jnp.inf, cur)
    exps = [jnp.exp(v - vals[0]) for v in vals]
    den = exps[0] + exps[1] + exps[2] + exps[3]
    sel = jnp.zeros(logits.shape, F32)
    for hot in hots:
        sel = sel + jnp.where(hot, 1.0, 0.0)
    ri = lax.broadcasted_iota(jnp.int32, (tm, tm), 0)
    ci = lax.broadcasted_iota(jnp.int32, (tm, tm), 1)
    tri = jnp.where(ci < ri, 1.0, 0.0).astype(BF16)
    rank = _dot(tri, sel.astype(BF16)) + run_ref[0:1, :]
    lane_f = lane.astype(F32)
    route = jnp.zeros(logits.shape, F32)
    for k in range(TOP_K):
        e_k = jnp.sum(jnp.where(hots[k], lane_f, 0.0), axis=-1, keepdims=True)
        r_k = jnp.sum(jnp.where(hots[k], rank, 0.0), axis=-1, keepdims=True)
        route = jnp.where(lane == k, e_k, route)
        route = jnp.where(lane == TOP_K + k, r_k, route)
        route = jnp.where(lane == 2 * TOP_K + k, exps[k] / den, route)
    route_ref[...] = route
    total = run_ref[...] + jnp.sum(sel, axis=0, keepdims=True)
    run_ref[...] = total
    cnt_ref[...] = total


def _post_call(l, merged, x, wo, modt, norm2, w_router, b_router, E, tm):
    T, D = x.shape

    def modc(k):
        return pl.BlockSpec((None, 1, D), lambda i: (i, 0, k))

    return pl.pallas_call(
        functools.partial(_post_kernel, E=E, tm=tm),
        grid=(T // tm,),
        in_specs=[
            pl.BlockSpec((tm, D), lambda i: (i, 0)),
            pl.BlockSpec((tm, D), lambda i: (i, 0)),
            pl.BlockSpec((None, D, D), lambda i: (l, 0, 0)),
            modc(2), modc(4), modc(3),
            pl.BlockSpec((None, 1, D), lambda i: (l, 0, 0)),
            pl.BlockSpec((None, D, LANES), lambda i: (l, 0, 0)),
            pl.BlockSpec((None, 1, LANES), lambda i: (l, 0, 0)),
        ],
        out_specs=[
            pl.BlockSpec((tm, D), lambda i: (i, 0)),
            pl.BlockSpec((tm, D), lambda i: (i, 0)),
            pl.BlockSpec((tm, LANES), lambda i: (i, 0)),
            pl.BlockSpec((COND_ROWS, LANES), lambda i: (0, 0)),
        ],
        out_shape=[
            jax.ShapeDtypeStruct((T, D), F32),
            jax.ShapeDtypeStruct((T, D), F32),
            jax.ShapeDtypeStruct((T, LANES), F32),
            jax.ShapeDtypeStruct((COND_ROWS, LANES), F32),
        ],
        scratch_shapes=[pltpu.VMEM((COND_ROWS, LANES), F32)],
        compiler_params=_cp(("arbitrary",)),
        name="out_proj_router",
    )(merged, x, wo, modt, modt, modt, norm2, w_router, b_router)


def _dispatch_kernel(dest_ref, pstart_ref, pcnt_ref, h_ref, xs_hbm, sem, *, tm, E):
    i = pl.program_id(0)

    def row_copy(r, dst):
        return pltpu.make_async_copy(h_ref.at[pl.ds(r, 1)], xs_hbm.at[pl.ds(dst, 1)], sem)

    def body(r, c):
        t = i * tm + r
        for k in range(TOP_K):
            row_copy(r, dest_ref[t * TOP_K + k]).start(priority=k % 2)
        return c

    lax.fori_loop(0, tm, body, 0, unroll=8)

    @pl.when(i == 0)
    def _():
        def pad_body(e, c):
            s = pstart_ref[e]
            n = pcnt_ref[e]
            lax.fori_loop(0, n, lambda r, c2: (row_copy(0, s + r).start(), c2)[1], 0)
            lax.fori_loop(0, n, lambda r, c2: (row_copy(0, 0).wait(), c2)[1], 0)
            return c

        lax.fori_loop(0, E, pad_body, 0)

    for _ in range(TOP_K):
        pltpu.make_async_copy(h_ref, xs_hbm.at[pl.ds(0, tm)], sem).wait()


def _dispatch_call(h2, dest, pstart, pcnt, rows, E, tm):
    T, D = h2.shape
    return pl.pallas_call(
        functools.partial(_dispatch_kernel, tm=tm, E=E),
        grid_spec=pltpu.PrefetchScalarGridSpec(
            num_scalar_prefetch=3,
            grid=(T // tm,),
            in_specs=[pl.BlockSpec((tm, D), lambda i, d, ps, pc: (i, 0))],
            out_specs=pl.BlockSpec(memory_space=pl.ANY),
            scratch_shapes=[pltpu.SemaphoreType.DMA],
        ),
        out_shape=jax.ShapeDtypeStruct((rows, D), F32),
        compiler_params=_cp(("arbitrary",)),
        name="expert_dispatch",
    )(dest, pstart, pcnt, h2)


def _expert_kernel(ge_ref, gs_ref, gn_ref, xs_hbm, wg_ref, wu_ref, wd_ref, bg_ref, bu_ref, bd_ref,
                   os_hbm, xland, xb, ybuf, wgb, wub, wdb, sem_in, sem_out, *, NF, G):
    g = pl.program_id(0)
    f = pl.program_id(1)
    n = gn_ref[g]
    start = gs_ref[g]
    g_next = jnp.minimum(g + 1, G - 1)
    n_next = jnp.where(g + 1 < G, gn_ref[g_next], 0)
    start_next = gs_ref[g_next]
    g_prev = jnp.maximum(g - 1, 0)
    n_prev = jnp.where(g > 0, gn_ref[g_prev], 0)
    start_prev = gs_ref[g_prev]

    def chunk_rows(s, c):
        return pl.ds(pl.multiple_of(s + c * EXPERT_CHUNK, EXPERT_CHUNK), EXPERT_CHUNK)

    def x_copy(s, c):
        return pltpu.make_async_copy(xs_hbm.at[chunk_rows(s, c)], xland.at[c], sem_in.at[c])

    def y_copy(s, c):
        return pltpu.make_async_copy(ybuf.at[c], os_hbm.at[chunk_rows(s, c)], sem_out.at[c])

    def for_chunks(cnt, fn, first=0):
        for c in range(first, GROUP_CHUNKS):
            pl.when(c < cnt)(functools.partial(fn, c))

    def load_chunk(c):
        x_copy(start, c).wait()
        xb[c] = xland[c].astype(BF16)

    def init_chunk(c):
        ybuf[c] = jnp.broadcast_to(bd_ref[...], ybuf.shape[1:])

    def wait_prev_rows():
        for_chunks(n_prev, lambda c: y_copy(start_prev, c).wait())

    def compute_chunk(c):
        x = xb[c]
        gt = jnp.minimum(_dot(x, wgb[...]) + bg_ref[...], SWIGLU_LIMIT)
        up = jnp.clip(_dot(x, wub[...]) + bu_ref[...], -SWIGLU_LIMIT, SWIGLU_LIMIT)
        act = ((up + 1.0) * gt * _sigmoid(SWIGLU_ALPHA * gt)).astype(BF16)
        ybuf[c] += _dot(act, wdb[...])

    @pl.when((f == 0) & (g == 0))
    def _():
        for_chunks(n, lambda c: x_copy(start, c).start())

    @pl.when((f == 0) & (n == 0))
    def _():
        wait_prev_rows()

    @pl.when(n > 0)
    def _():
        wgb[...] = wg_ref[...].astype(BF16)
        wub[...] = wu_ref[...].astype(BF16)
        wdb[...] = wd_ref[...].astype(BF16)

        @pl.when(f == 0)
        def _():
            for_chunks(n, load_chunk)
            wait_prev_rows()
            for_chunks(n, init_chunk)

        @pl.when(f == NF - 1)
        def _():
            for_chunks(n_next, lambda c: x_copy(start_next, c).start())

        compute_chunk(0)
        for_chunks(n, compute_chunk, first=1)

        @pl.when(f == NF - 1)
        def _():
            for_chunks(n, lambda c: y_copy(start, c).start())

            @pl.when(g == G - 1)
            def _():
                for_chunks(n, lambda c: y_copy(start, c).wait())


def _expert_call(l, xs, ge, gs, gn, w_gate, w_up, w_down, b_gate, b_up, b_down, tf):
    rows, D = xs.shape
    F = w_gate.shape[-1]
    NF = F // tf
    G = ge.shape[0]

    def fsel(f, gn, g):
        return jnp.where(gn[g] > 0, f, NF - 1)

    return pl.pallas_call(
        functools.partial(_expert_kernel, NF=NF, G=G),
        grid_spec=pltpu.PrefetchScalarGridSpec(
            num_scalar_prefetch=3,
            grid=(G, NF),
            in_specs=[
                pl.BlockSpec(memory_space=pl.ANY),
                pl.BlockSpec((None, None, D, tf), lambda g, f, ge, gs, gn: (l, ge[g], 0, fsel(f, gn, g))),
                pl.BlockSpec((None, None, D, tf), lambda g, f, ge, gs, gn: (l, ge[g], 0, fsel(f, gn, g))),
                pl.BlockSpec((None, None, tf, D), lambda g, f, ge, gs, gn: (l, ge[g], fsel(f, gn, g), 0)),
                pl.BlockSpec((None, None, 1, tf), lambda g, f, ge, gs, gn: (l, ge[g], 0, fsel(f, gn, g))),
                pl.BlockSpec((None, None, 1, tf), lambda g, f, ge, gs, gn: (l, ge[g], 0, fsel(f, gn, g))),
                pl.BlockSpec((None, None, 1, D), lambda g, f, ge, gs, gn: (l, ge[g], 0, 0)),
            ],
            out_specs=pl.BlockSpec(memory_space=pl.ANY),
            scratch_shapes=[
                pltpu.VMEM((GROUP_CHUNKS, EXPERT_CHUNK, D), F32),
                pltpu.VMEM((GROUP_CHUNKS, EXPERT_CHUNK, D), BF16),
                pltpu.VMEM((GROUP_CHUNKS, EXPERT_CHUNK, D), F32),
                pltpu.VMEM((D, tf), BF16),
                pltpu.VMEM((D, tf), BF16),
                pltpu.VMEM((tf, D), BF16),
                pltpu.SemaphoreType.DMA((GROUP_CHUNKS,)),
                pltpu.SemaphoreType.DMA((GROUP_CHUNKS,)),
            ],
        ),
        out_shape=jax.ShapeDtypeStruct((rows, D), F32),
        compiler_params=_cp(("arbitrary", "arbitrary")),
        name="grouped_experts",
    )(ge, gs, gn, xs, w_gate, w_up, w_down, b_gate, b_up, b_down)


def _combine_kernel(dest_ref, os_hbm, x1_ref, g2_ref, route_ref, o_ref, buf, sem, *, tm):
    i = pl.program_id(0)
    nt = pl.num_programs(0)

    def issue(tile, slot):
        def body(r, c):
            t = tile * tm + r
            for k in range(TOP_K):
                pltpu.make_async_copy(os_hbm.at[pl.ds(dest_ref[t * TOP_K + k], 1)],
                                      buf.at[slot, k, pl.ds(r, 1)], sem.at[slot]).start(priority=k % 2)
            return c

        lax.fori_loop(0, tm, body, 0, unroll=8)

    @pl.when(i == 0)
    def _():
        issue(0, 0)

    @pl.when(i + 1 < nt)
    def _():
        issue(i + 1, (i + 1) % 2)

    slot = i % 2
    for k in range(TOP_K):
        pltpu.make_async_copy(os_hbm.at[pl.ds(0, tm)], buf.at[slot, k], sem.at[slot]).wait()
    route = route_ref[...]
    y = route[:, 2 * TOP_K:2 * TOP_K + 1] * buf[slot, 0]
    for k in range(1, TOP_K):
        y = y + route[:, 2 * TOP_K + k:2 * TOP_K + k + 1] * buf[slot, k]
    o_ref[...] = x1_ref[...] + g2_ref[...] * y


def _combine_call(dest, os_, x1, modt, route, tm):
    T, D = x1.shape
    return pl.pallas_call(
        functools.partial(_combine_kernel, tm=tm),
        grid_spec=pltpu.PrefetchScalarGridSpec(
            num_scalar_prefetch=1,
            grid=(T // tm,),
            in_specs=[
                pl.BlockSpec(memory_space=pl.ANY),
                pl.BlockSpec((tm, D), lambda i, d: (i, 0)),
                pl.BlockSpec((None, 1, D), lambda i, d: (i, 0, 5)),
                pl.BlockSpec((tm, LANES), lambda i, d: (i, 0)),
            ],
            out_specs=pl.BlockSpec((tm, D), lambda i, d: (i, 0)),
            scratch_shapes=[pltpu.VMEM((2, TOP_K, tm, D), F32), pltpu.SemaphoreType.DMA((2,))],
        ),
        out_shape=jax.ShapeDtypeStruct((T, D), F32),
        compiler_params=_cp(("arbitrary",)),
        name="expert_combine",
    )(dest, os_, x1, modt, route)


def _rope_tables(pos_a, pos_b, half, reps):
    freqs = ROPE_BASE ** (-jnp.arange(half, dtype=F32) / half)

    def blk(pos):
        ang = pos.astype(F32)[:, None] * freqs[None, :]
        c, s = jnp.cos(ang), jnp.sin(ang)
        return jnp.concatenate([c, c], -1), jnp.concatenate([-s, s], -1)

    ca, sa = blk(pos_a)
    cb, sb = blk(pos_b)
    cos = jnp.tile(jnp.concatenate([ca, cb], -1), (1, reps))
    sin = jnp.tile(jnp.concatenate([sa, sb], -1), (1, reps))
    return cos, sin


def _cond_tiles(T, Tp, Ss, tm):
    starts = np.arange(0, T, tm)
    return np.where(starts < Tp, 0, 1 + np.maximum(starts - Tp, 0) // Ss).astype(np.int32)


def _pick_tile(limit, *sizes):
    t = ROW_ALIGN
    while t * 2 <= limit and all(s % (t * 2) == 0 for s in sizes):
        t *= 2
    return t


def _route_plan(route, cnt, E, G):
    idx = route[:, 0:TOP_K].astype(jnp.int32)
    rank = route[:, TOP_K:2 * TOP_K].astype(jnp.int32)
    counts = cnt[0, :E].astype(jnp.int32)
    nch = (counts + EXPERT_CHUNK - 1) // EXPERT_CHUNK
    padded = nch * EXPERT_CHUNK
    base = jnp.cumsum(padded) - padded
    dest = (base[idx] + rank).reshape(-1)
    ngr = (nch + GROUP_CHUNKS - 1) // GROUP_CHUNKS
    gend = jnp.cumsum(ngr)
    gid = jnp.arange(G, dtype=jnp.int32)
    valid = gid < gend[-1]
    ge = jnp.minimum(jnp.sum((gid[:, None] >= gend[None, :]).astype(jnp.int32), axis=1), E - 1)
    j = gid - (gend - ngr)[ge]
    gs = base[ge] + j * (GROUP_CHUNKS * EXPERT_CHUNK)
    gn = jnp.clip(nch[ge] - j * GROUP_CHUNKS, 0, GROUP_CHUNKS)
    last = jnp.maximum(gend[-1] - 1, 0)
    ge = jnp.where(valid, ge, ge[last])
    gs = jnp.where(valid, gs, 0)
    gn = jnp.where(valid, gn, 0)
    return (dest.astype(jnp.int32), (base + counts).astype(jnp.int32), (padded - counts).astype(jnp.int32),
            ge.astype(jnp.int32), gs.astype(jnp.int32), gn.astype(jnp.int32))


def kernel(x_prompt, x_sample, cache_diff_k, cache_diff_v, cache_mla_ckv, cache_mla_krope, state_ret_fwd, state_ret_bwd, c, c_ctx, w_mod, b_mod, norm1, norm2, w_in, diff_q_gain, diff_k_gain, diff_lambda, diff_subln, mla_q_gain, w_uq, mla_kv_gain, w_ukv, mla_qn_gain, mla_kn_gain, ret_decay, w_br_diff, w_br_mla, w_br_ret, w_o, w_router, b_router, w_gate, b_gate, w_up, b_up, w_down, b_down):
    Bp, Sp, D = x_prompt.shape
    Bs, Ss, _ = x_sample.shape
    L = w_mod.shape[0]
    P = cache_diff_k.shape[2]
    E = w_router.shape[-1]
    F = w_gate.shape[-1]
    Tp, Ts = Bp * Sp, Bs * Ss
    T = Tp + Ts
    assert 1 + Bs <= COND_ROWS and E <= LANES
    assert Tp % ROW_ALIGN == 0 and Ss % ROW_ALIGN == 0 and Tp % Ss == 0 and Sp % RET_CHUNK == 0
    assert (3 * D) % 512 == 0 and F % 512 == 0

    off = 3 * D
    o_gl = sum((512, 512, 512, 512, 256, 64, 256, 256, 512, 512, 512))
    w_in_b = w_in.astype(BF16)
    w_in_p = jnp.concatenate([
        w_in_b[..., o_gl:], w_in_b[..., 0:2304], w_in_b[..., 2368:2880], w_in_b[..., 2304:2368],
        jnp.zeros((L, D, 192), BF16), w_in_b[..., 2880:o_gl]], axis=-1)
    assert w_in_p.shape[-1] == off + _SEC_TOTAL
    cond = jnp.zeros((COND_ROWS, D), F32).at[0].set(c_ctx).at[1:1 + Bs].set(c)
    wuq_p = jnp.pad(w_uq, ((0, 0), (0, 0), (0, 0), (0, MLA_NOPE + MLA_VD - MLA_QK))).reshape(
        L, MLA_Q_RANK, -1).astype(BF16)
    wukv_p = w_ukv.reshape(L, MLA_KV_RANK, -1).astype(BF16)
    pad_qk = ((0, 0), (0, MLA_NOPE + MLA_VD - MLA_QK))
    mla_w = (wuq_p, wukv_p, mla_q_gain[:, None, :], mla_kv_gain[:, None, :],
             jnp.pad(mla_qn_gain, pad_qk)[:, None, :], jnp.pad(mla_kn_gain, pad_qk)[:, None, :])
    diff_g = (jnp.tile(diff_q_gain, (1, 2))[:, None, :], jnp.tile(diff_k_gain, (1, 2))[:, None, :],
              diff_subln[:, None, :])
    wd_b, wm_b, wr_b, wo_b = (w.astype(BF16) for w in (w_br_diff, w_br_mla, w_br_ret, w_o))
    w_router_p = jnp.pad(w_router, ((0, 0), (0, 0), (0, LANES - E)))
    b_router_p = jnp.pad(b_router, ((0, 0), (0, LANES - E)))[:, None, :]
    lam_vec = diff_lambda.astype(F32)
    lam_init = jnp.asarray([0.8 - 0.6 * math.exp(-0.3 * l) for l in range(L)], F32)
    lam = (jnp.exp(jnp.sum(lam_vec[:, 0] * lam_vec[:, 1], -1))
           - jnp.exp(jnp.sum(lam_vec[:, 2] * lam_vec[:, 3], -1)) + lam_init)
    log_g = jax.nn.log_sigmoid(ret_decay.astype(F32)).reshape(-1)
    cdk = cache_diff_k.reshape(Bs, L, P, DIFF_HEADS * 2 * DIFF_HD)
    cdv = cache_diff_v.reshape(Bs, L, P, DIFF_HEADS * DIFF_VD)
    ckr = jnp.pad(cache_mla_krope, ((0, 0), (0, 0), (0, 0), (0, LANES - MLA_ROPE)))
    t_pos = jnp.arange(Ss, dtype=jnp.int32)
    tabs_ax = _rope_tables(t_pos // GRID_W, t_pos % GRID_W, 16, 2)
    tabs_1d = _rope_tables(t_pos, t_pos, 32, 1)
    bg4, bu4, bd4 = b_gate[:, :, None, :], b_up[:, :, None, :], b_down[:, :, None, :]

    tm_p = _pick_tile(1024, Tp, Ss)
    tm_r = ROW_ALIGN
    tq = ROW_ALIGN
    rows_sorted = T * TOP_K + E * EXPERT_CHUNK
    G = E + (T * TOP_K) // (GROUP_CHUNKS * EXPERT_CHUNK)
    tf = 512

    mod = _mod_call(cond, w_mod, b_mod)
    x = jnp.concatenate([x_prompt.reshape(Tp, D), x_sample.reshape(Ts, D)], axis=0)
    outs = [[] for _ in range(6)]
    for l in range(L):
        modt_p = mod[l][_cond_tiles(T, Tp, Ss, tm_p)][:, None, :]
        modt_r = mod[l][_cond_tiles(T, Tp, Ss, tm_r)][:, None, :]
        proj = _proj_call(l, x, modt_p, norm1[:, None, :], w_in_p, tm_p, 512)

        yd_p, kn_p = _diff_call(l, proj, off, lam, diff_g, Bp, Sp, 0, min(tq, Sp), False)
        (yd_s,) = _diff_call(l, proj, off, lam, diff_g, Bs, Ss, Tp, tq, True, (cdk, cdv), tabs_ax)
        ym_p, ckv_p = _mla_call(l, proj, off, mla_w, Bp, Sp, 0, min(tq, Sp), False)
        (ym_s,) = _mla_call(l, proj, off, mla_w, Bs, Ss, Tp, tq, True, (cache_mla_ckv, ckr), tabs_ax)
        yr_p, sf_p, sb_p = _ret_call(l, proj, off, log_g, Bp, Sp, 0, False)
        (yr_s,) = _ret_call(l, proj, off, log_g, Bs, Ss, Tp, True, (state_ret_fwd, state_ret_bwd), tabs_1d)

        yd = jnp.concatenate([yd_p, yd_s], axis=0)
        ym = jnp.concatenate([ym_p, ym_s], axis=0)
        yr = jnp.concatenate([yr_p, yr_s], axis=0)
        merged = _merge_call(l, yd, ym, yr, proj, wd_b, wm_b, wr_b, tm_r)
        x1, h2, route, cnt = _post_call(l, merged, x, wo_b, modt_r, norm2[:, None, :],
                                        w_router_p, b_router_p, E, tm_r)
        dest, pstart, pcnt, ge, gs, gn = _route_plan(route, cnt, E, G)
        xs = _dispatch_call(h2, dest, pstart, pcnt, rows_sorted, E, tm_r)
        os_ = _expert_call(l, xs, ge, gs, gn, w_gate, w_up, w_down, bg4, bu4, bd4, tf)
        x = _combine_call(dest, os_, x1, modt_r, route, tm_r)

        dv = proj[:Tp, off + _SEC["DV"]:off + _SEC["DV"] + DIFF_HEADS * DIFF_VD]
        kr = proj[:Tp, off + _SEC["MKR"]:off + _SEC["MKR"] + MLA_ROPE]
        for lst, v in zip(outs, (kn_p.reshape(Bp, Sp, DIFF_HEADS, 2 * DIFF_HD),
                                 dv.reshape(Bp, Sp, DIFF_HEADS, DIFF_VD),
                                 ckv_p.reshape(Bp, Sp, MLA_KV_RANK), kr.reshape(Bp, Sp, MLA_ROPE),
                                 sf_p, sb_p)):
            lst.append(v)

    y_prompt = x[:Tp].reshape(Bp, Sp, D)
    y_sample = x[Tp:].reshape(Bs, Ss, D)
    return (y_prompt, y_sample) + tuple(jnp.stack(o, axis=1) for o in outs)
```

```python
import functools
import math

import jax
import jax.numpy as jnp
import numpy as np
from jax import lax
from jax.experimental import pallas as pl
from jax.experimental.pallas import tpu as pltpu

F32 = jnp.float32
BF16 = jnp.bfloat16

GRID_W = 64
DIFF_HEADS = 4
DIFF_HD = 64
DIFF_VD = 128
MLA_HEADS = 8
MLA_Q_RANK = 512
MLA_KV_RANK = 256
MLA_NOPE = 128
MLA_ROPE = 64
MLA_VD = 128
MLA_QK = MLA_NOPE + MLA_ROPE
RET_HEADS = 4
RET_QK = 64
RET_VD = 128
RET_CHUNK = 128
TOP_K = 4
SWIGLU_LIMIT = 7.0
SWIGLU_ALPHA = 1.702
ROPE_BASE = 10000.0
EPS = 1e-6

LANES = 128
ROW_ALIGN = 256
EXPERT_CHUNK = 256
GROUP_CHUNKS = 4
COND_ROWS = 8
VMEM_LIMIT = 56 * 1024 * 1024

_SEC = dict(DQ=0, DK=512, DV=1024, MQ=1536, MKV=2048, RQ=2304, RK=2560, MKR=2816,
            RV=3072, RGF=3584, RGB=4096)
_SEC_TOTAL = 4608


def _cp(sem, vmem=VMEM_LIMIT):
    return pltpu.CompilerParams(dimension_semantics=sem, vmem_limit_bytes=vmem)


def _lane_iota(shape):
    return lax.broadcasted_iota(jnp.int32, shape, len(shape) - 1)


def _rms(x, n=None):
    n = x.shape[-1] if n is None else n
    return x * lax.rsqrt(jnp.sum(x * x, axis=-1, keepdims=True) / n + EPS)


def _dot(a, b):
    return jnp.dot(a, b, preferred_element_type=F32)


def _dot_nt(a, b):
    return lax.dot_general(a, b, (((1,), (1,)), ((), ())), preferred_element_type=F32)


def _softmax(s):
    e = jnp.exp(s - jnp.max(s, axis=-1, keepdims=True))
    return e / jnp.sum(e, axis=-1, keepdims=True)


def _sigmoid(x):
    return 1.0 / (1.0 + jnp.exp(-x))


def _rope(x, cos, sin, half):
    lane = _lane_iota(x.shape)
    up = pltpu.roll(x, LANES - half, 1)
    dn = pltpu.roll(x, half, 1)
    sw = jnp.where(lane % (2 * half) < half, up, dn)
    return x * cos + sw * sin


def _norm_halves(x, gain):
    lane = _lane_iota(x.shape)
    lo = lane < DIFF_HD
    x2 = x * x
    ss_lo = jnp.sum(jnp.where(lo, x2, 0.0), axis=-1, keepdims=True)
    ss_hi = jnp.sum(jnp.where(lo, 0.0, x2), axis=-1, keepdims=True)
    inv = jnp.where(lo, lax.rsqrt(ss_lo / DIFF_HD + EPS), lax.rsqrt(ss_hi / DIFF_HD + EPS))
    return x * inv * gain


def _mod_kernel(c_ref, w_ref, b_ref, o_ref):
    c = c_ref[...]
    a = (c * _sigmoid(c)).astype(BF16)
    o_ref[...] = _dot(a, w_ref[...].astype(BF16)) + b_ref[...]


def _mod_call(cond, w_mod, b_mod):
    L, D, N = w_mod.shape
    tn = 1024 if N % 1024 == 0 else 512
    return pl.pallas_call(
        _mod_kernel,
        grid=(L, N // tn),
        in_specs=[
            pl.BlockSpec((COND_ROWS, D), lambda l, j: (0, 0)),
            pl.BlockSpec((None, D, tn), lambda l, j: (l, 0, j)),
            pl.BlockSpec((None, 1, tn), lambda l, j: (l, 0, j)),
        ],
        out_specs=pl.BlockSpec((None, COND_ROWS, tn), lambda l, j: (l, 0, j)),
        out_shape=jax.ShapeDtypeStruct((L, COND_ROWS, N), F32),
        compiler_params=_cp(("arbitrary", "arbitrary")),
        name="adaln_mod",
    )(cond, w_mod, b_mod.reshape(L, 1, N))


def _proj_kernel(x_ref, sc_ref, sh_ref, g_ref, w_ref, o_ref, h_ref):
    @pl.when(pl.program_id(1) == 0)
    def _():
        h = _rms(x_ref[...]) * g_ref[...]
        h_ref[...] = (h * (1.0 + sc_ref[...]) + sh_ref[...]).astype(BF16)

    o_ref[...] = _dot(h_ref[...], w_ref[...])


def _proj_call(l, x, modt, norm1, w_in_p, tm, tn):
    T, D = x.shape
    PC = w_in_p.shape[-1]
    return pl.pallas_call(
        _proj_kernel,
        grid=(T // tm, PC // tn),
        in_specs=[
            pl.BlockSpec((tm, D), lambda i, j: (i, 0)),
            pl.BlockSpec((None, 1, D), lambda i, j: (i, 0, 1)),
            pl.BlockSpec((None, 1, D), lambda i, j: (i, 0, 0)),
            pl.BlockSpec((None, 1, D), lambda i, j: (l, 0, 0)),
            pl.BlockSpec((None, D, tn), lambda i, j: (l, 0, j)),
        ],
        out_specs=pl.BlockSpec((tm, tn), lambda i, j: (i, j)),
        out_shape=jax.ShapeDtypeStruct((T, PC), F32),
        scratch_shapes=[pltpu.VMEM((tm, D), BF16)],
        compiler_params=_cp(("arbitrary", "arbitrary")),
        name="norm_proj",
    )(x, modt, modt, norm1, w_in_p)


def _diff_kernel(*refs, latent, S, P, lam_init, l):
    if latent:
        (lam_ref, q_ref, k_ref, v_ref, qg_ref, kg_ref, sg_ref, ck_ref, cv_ref,
         cosq_ref, sinq_ref, cosk_ref, sink_ref, y_ref, kall, vall) = refs
    else:
        (lam_ref, q_ref, k_ref, v_ref, qg_ref, kg_ref, sg_ref, y_ref, kn_ref, kall, vall) = refs

    @pl.when(pl.program_id(1) == 0)
    def _():
        for h in range(DIFF_HEADS):
            sl = slice(LANES * h, LANES * (h + 1))
            kn = _norm_halves(k_ref[:, sl], kg_ref[...])
            if latent:
                kn = _rope(kn, cosk_ref[...], sink_ref[...], 16)
            else:
                kn_ref[:, sl] = kn
            kall[0:S, sl] = kn.astype(BF16)
        vall[0:S, :] = v_ref[...].astype(BF16)
        if latent:
            kall[S:S + P, :] = ck_ref[...].astype(BF16)
            vall[S:S + P, :] = cv_ref[...].astype(BF16)

    lam = lam_ref[l]
    scale = DIFF_HD ** -0.5
    for h in range(DIFF_HEADS):
        sl = slice(LANES * h, LANES * (h + 1))
        qn = _norm_halves(q_ref[:, sl], qg_ref[...])
        if latent:
            qn = _rope(qn, cosq_ref[...], sinq_ref[...], 16)
        lo = _lane_iota(qn.shape) < DIFF_HD
        q1 = jnp.where(lo, qn, 0.0).astype(BF16)
        q2 = jnp.where(lo, 0.0, qn).astype(BF16)
        kh = kall[:, sl]
        p = _softmax(_dot_nt(q1, kh) * scale) - lam * _softmax(_dot_nt(q2, kh) * scale)
        o = _dot(p.astype(BF16), vall[:, sl])
        y_ref[:, sl] = (_rms(o) * sg_ref[...] * (1.0 - lam_init)).astype(BF16)


def _diff_call(l, proj, off, lam, gains, B, S, row0, tq, latent, cache=None, tabs=None):
    qg, kg, sg = gains
    W = DIFF_HEADS * LANES
    nq = S // tq
    P = cache[0].shape[2] if latent else 0
    lam_init = 0.8 - 0.6 * math.exp(-0.3 * l)
    cq, ck, cv = ((off + _SEC[n]) // W for n in ("DQ", "DK", "DV"))
    in_specs = [
        pl.BlockSpec(memory_space=pltpu.SMEM),
        pl.BlockSpec((tq, W), lambda b, i: (row0 // tq + b * nq + i, cq)),
        pl.BlockSpec((S, W), lambda b, i: (row0 // S + b, ck)),
        pl.BlockSpec((S, W), lambda b, i: (row0 // S + b, cv)),
        pl.BlockSpec((None, 1, LANES), lambda b, i: (l, 0, 0)),
        pl.BlockSpec((None, 1, LANES), lambda b, i: (l, 0, 0)),
        pl.BlockSpec((None, 1, LANES), lambda b, i: (l, 0, 0)),
    ]
    args = [lam, proj, proj, proj, qg, kg, sg]
    out_shape = [jax.ShapeDtypeStruct((B * S, W), BF16)]
    out_specs = [pl.BlockSpec((tq, W), lambda b, i: (b * nq + i, 0))]
    if latent:
        cos, sin = tabs
        in_specs += [
            pl.BlockSpec((None, None, P, W), lambda b, i: (b, l, 0, 0)),
            pl.BlockSpec((None, None, P, W), lambda b, i: (b, l, 0, 0)),
            pl.BlockSpec((tq, LANES), lambda b, i: (i, 0)),
            pl.BlockSpec((tq, LANES), lambda b, i: (i, 0)),
            pl.BlockSpec((S, LANES), lambda b, i: (0, 0)),
            pl.BlockSpec((S, LANES), lambda b, i: (0, 0)),
        ]
        args += [cache[0], cache[1], cos, sin, cos, sin]
    else:
        out_shape.append(jax.ShapeDtypeStruct((B * S, W), F32))
        out_specs.append(pl.BlockSpec((S, W), lambda b, i: (b, 0)))
    return pl.pallas_call(
        functools.partial(_diff_kernel, latent=latent, S=S, P=P, lam_init=lam_init, l=l),
        grid=(B, nq),
        in_specs=in_specs,
        out_specs=out_specs,
        out_shape=out_shape,
        scratch_shapes=[pltpu.VMEM((S + P, W), BF16), pltpu.VMEM((S + P, W), BF16)],
        compiler_params=_cp(("arbitrary", "arbitrary")),
        name="diff_attn_lat" if latent else "diff_attn_ctx",
    )(*args)


def _mla_kernel(*refs, latent, S, P):
    if latent:
        (mq_ref, mkv_ref, mkr_ref, wuq_ref, wukv_ref, qg_ref, kvg_ref, qng_ref, kng_ref,
         cckv_ref, ckr_ref, cosq_ref, sinq_ref, cosk_ref, sink_ref, y_ref, kall, vall) = refs
    else:
        (mq_ref, mkv_ref, mkr_ref, wuq_ref, wukv_ref, qg_ref, kvg_ref, qng_ref, kng_ref,
         y_ref, ckv_ref, kall, vall) = refs
    HW = MLA_NOPE + MLA_VD

    @pl.when(pl.program_id(1) == 0)
    def _():
        ckv = _rms(mkv_ref[...]) * kvg_ref[...]
        if not latent:
            ckv_ref[...] = ckv
        segs = [(0, S, ckv, mkr_ref[...], latent)]
        if latent:
            segs.append((S, P, cckv_ref[...], ckr_ref[...], False))
        for r0, R, cv, kr, rot in segs:
            kv = _dot(cv.astype(BF16), wukv_ref[...])
            kr_ss = jnp.sum(kr * kr, axis=-1, keepdims=True)
            krg = kr * kng_ref[:, MLA_NOPE:]
            if rot:
                krg = _rope(krg, cosk_ref[...], sink_ref[...], 16)
            for h in range(MLA_HEADS):
                kn = kv[:, HW * h:HW * h + MLA_NOPE]
                ss = jnp.sum(kn * kn, axis=-1, keepdims=True) + kr_ss
                r = lax.rsqrt(ss / MLA_QK + EPS)
                kall[h, r0:r0 + R, 0:MLA_NOPE] = (kn * r * kng_ref[:, 0:MLA_NOPE]).astype(BF16)
                kall[h, r0:r0 + R, MLA_NOPE:] = (krg * r).astype(BF16)
                vall[h, r0:r0 + R, :] = kv[:, HW * h + MLA_NOPE:HW * (h + 1)].astype(BF16)

    cq = (_rms(mq_ref[...]) * qg_ref[...]).astype(BF16)
    qm = _dot(cq, wuq_ref[...])
    scale = MLA_QK ** -0.5
    for h in range(MLA_HEADS):
        qn = _rms(qm[:, HW * h:HW * (h + 1)], MLA_QK) * qng_ref[...]
        if latent:
            qr = _rope(qn[:, MLA_NOPE:], cosq_ref[...], sinq_ref[...], 16)
            qn = jnp.concatenate([qn[:, 0:MLA_NOPE], qr], axis=-1)
        p = _softmax(_dot_nt(qn.astype(BF16), kall[h]) * scale)
        y_ref[:, MLA_VD * h:MLA_VD * (h + 1)] = _dot(p.astype(BF16), vall[h]).astype(BF16)


def _mla_call(l, proj, off, weights, B, S, row0, tq, latent, cache=None, tabs=None):
    wuq, wukv, qg, kvg, qng, kng = weights
    nq = S // tq
    P = cache[0].shape[2] if latent else 0
    HW = MLA_NOPE + MLA_VD
    NW = MLA_HEADS * HW
    c_mq = (off + _SEC["MQ"]) // MLA_Q_RANK
    c_mkv = (off + _SEC["MKV"]) // MLA_KV_RANK
    c_mkr = (off + _SEC["MKR"]) // LANES

    def lay(shape):
        return pl.BlockSpec((None,) + shape, lambda b, i: (l,) + (0,) * len(shape))

    in_specs = [
        pl.BlockSpec((tq, MLA_Q_RANK), lambda b, i: (row0 // tq + b * nq + i, c_mq)),
        pl.BlockSpec((S, MLA_KV_RANK), lambda b, i: (row0 // S + b, c_mkv)),
        pl.BlockSpec((S, LANES), lambda b, i: (row0 // S + b, c_mkr)),
        lay((MLA_Q_RANK, NW)), lay((MLA_KV_RANK, NW)),
        lay((1, MLA_Q_RANK)), lay((1, MLA_KV_RANK)), lay((1, HW)), lay((1, HW)),
    ]
    args = [proj, proj, proj, wuq, wukv, qg, kvg, qng, kng]
    out_shape = [jax.ShapeDtypeStruct((B * S, MLA_HEADS * MLA_VD), BF16)]
    out_specs = [pl.BlockSpec((tq, MLA_HEADS * MLA_VD), lambda b, i: (b * nq + i, 0))]
    if latent:
        cos, sin = tabs
        in_specs += [
            pl.BlockSpec((None, None, P, MLA_KV_RANK), lambda b, i: (b, l, 0, 0)),
            pl.BlockSpec((None, None, P, LANES), lambda b, i: (b, l, 0, 0)),
            pl.BlockSpec((tq, LANES), lambda b, i: (i, 0)),
            pl.BlockSpec((tq, LANES), lambda b, i: (i, 0)),
            pl.BlockSpec((S, LANES), lambda b, i: (0, 0)),
            pl.BlockSpec((S, LANES), lambda b, i: (0, 0)),
        ]
        args += [cache[0], cache[1], cos, sin, cos, sin]
    else:
        out_shape.append(jax.ShapeDtypeStruct((B * S, MLA_KV_RANK), F32))
        out_specs.append(pl.BlockSpec((S, MLA_KV_RANK), lambda b, i: (b, 0)))
    return pl.pallas_call(
        functools.partial(_mla_kernel, latent=latent, S=S, P=P),
        grid=(B, nq),
        in_specs=in_specs,
        out_specs=out_specs,
        out_shape=out_shape,
        scratch_shapes=[pltpu.VMEM((MLA_HEADS, S + P, HW), BF16),
                        pltpu.VMEM((MLA_HEADS, S + P, MLA_VD), BF16)],
        compiler_params=_cp(("arbitrary", "arbitrary")),
        name="mla_lat" if latent else "mla_ctx",
    )(*args)


def _ret_kernel(*refs, latent, S, l):
    if latent:
        (lg_ref, rq_ref, rk_ref, rv_ref, gf_ref, gb_ref, cos_ref, sin_ref, s0f_ref, s0b_ref,
         y_ref, qs, ks, of, ob, st, dm, dq, dk) = refs
    else:
        (lg_ref, rq_ref, rk_ref, rv_ref, gf_ref, gb_ref, y_ref, sf_ref, sb_ref,
         qs, ks, of, ob, st, dm, dq, dk) = refs
    C = RET_CHUNK
    H = RET_HEADS
    nc = S // C

    @pl.when(pl.program_id(0) == 0)
    def _():
        ii = lax.broadcasted_iota(jnp.int32, (C, C), 0).astype(F32)
        jj = lax.broadcasted_iota(jnp.int32, (C, C), 1).astype(F32)
        row = lax.broadcasted_iota(jnp.int32, (C, LANES), 0).astype(F32)
        for d in range(2):
            for h in range(H):
                lg = lg_ref[(l * 2 + d) * H + h]
                rel = ii - jj if d == 0 else jj - ii
                dm[d * H + h] = jnp.where(rel >= 0, jnp.exp(jnp.maximum(rel, 0.0) * lg), 0.0)
                dq[d * H + h] = jnp.exp(((row + 1.0) if d == 0 else (C - row)) * lg)
                dk[d * H + h] = jnp.exp(((C - 1.0 - row) if d == 0 else row) * lg)

    for s in range(H // 2):
        ssl = slice(LANES * s, LANES * (s + 1))
        qslab = rq_ref[:, ssl]
        kslab = rk_ref[:, ssl] * (RET_QK ** -0.5)
        if latent:
            qslab = _rope(qslab, cos_ref[...], sin_ref[...], 32)
            kslab = _rope(kslab, cos_ref[...], sin_ref[...], 32)
        qs[:, ssl] = qslab
        ks[:, ssl] = kslab
    for d in range(2):
        for h in range(H):
            if latent:
                s0 = (s0f_ref if d == 0 else s0b_ref)[h]
                z = jnp.zeros_like(s0)
                st[d * H + h] = jnp.concatenate([s0, z] if h % 2 == 0 else [z, s0], axis=0)
            else:
                st[d * H + h] = jnp.zeros((LANES, RET_VD), F32)

    lo = _lane_iota((C, LANES)) < RET_QK
    one = jnp.ones((1, 1), F32)

    def body(n, carry):
        for d in range(2):
            r0 = pl.multiple_of((n if d == 0 else nc - 1 - n) * C, C)
            for h in range(H):
                i = d * H + h
                ssl = slice(LANES * (h // 2), LANES * (h // 2 + 1))
                vsl = slice(RET_VD * h, RET_VD * (h + 1))
                keep = lo if h % 2 == 0 else jnp.logical_not(lo)
                qc = jnp.where(keep, qs[pl.ds(r0, C), ssl], 0.0)
                kc = jnp.where(keep, ks[pl.ds(r0, C), ssl], 0.0)
                vc = rv_ref[pl.ds(r0, C), vsl].astype(BF16)
                sc = _dot_nt(qc.astype(BF16), kc.astype(BF16)) * dm[i]
                stv = st[i]
                o = _dot(sc.astype(BF16), vc) + _dot((qc * dq[i]).astype(BF16), stv.astype(BF16))
                (of if d == 0 else ob)[pl.ds(r0, C), vsl] = o
                kt = jnp.transpose(kc * dk[i]).astype(BF16)
                cdec = jnp.exp(one * (C * lg_ref[(l * 2 + d) * H + h]))
                st[i] = stv * cdec + _dot(kt, vc)
        return carry

    lax.fori_loop(0, nc, body, 0)

    for h in range(H):
        vsl = slice(RET_VD * h, RET_VD * (h + 1))
        if not latent:
            sf_ref[h] = st[h, RET_QK * (h % 2):RET_QK * (h % 2 + 1), :]
            sb_ref[h] = st[H + h, RET_QK * (h % 2):RET_QK * (h % 2 + 1), :]
        gf = gf_ref[:, vsl]
        gb = gb_ref[:, vsl]
        y = _rms(of[:, vsl]) * (gf * _sigmoid(gf)) + _rms(ob[:, vsl]) * (gb * _sigmoid(gb))
        y_ref[:, vsl] = y.astype(BF16)


def _ret_call(l, proj, off, lg, B, S, row0, latent, states=None, tabs=None):
    QW = RET_HEADS * RET_QK
    VW = RET_HEADS * RET_VD
    c_rq, c_rk = (off + _SEC["RQ"]) // QW, (off + _SEC["RK"]) // QW
    c_rv, c_gf, c_gb = ((off + _SEC[n]) // VW for n in ("RV", "RGF", "RGB"))
    rb = row0 // S
    in_specs = [
        pl.BlockSpec(memory_space=pltpu.SMEM),
        pl.BlockSpec((S, QW), lambda b: (rb + b, c_rq)),
        pl.BlockSpec((S, QW), lambda b: (rb + b, c_rk)),
        pl.BlockSpec((S, VW), lambda b: (rb + b, c_rv)),
        pl.BlockSpec((S, VW), lambda b: (rb + b, c_gf)),
        pl.BlockSpec((S, VW), lambda b: (rb + b, c_gb)),
    ]
    args = [lg, proj, proj, proj, proj, proj]
    out_shape = [jax.ShapeDtypeStruct((B * S, VW), BF16)]
    out_specs = [pl.BlockSpec((S, VW), lambda b: (b, 0))]
    st_block = (None, RET_HEADS, RET_QK, RET_VD)
    if latent:
        cos, sin = tabs
        in_specs += [
            pl.BlockSpec((S, LANES), lambda b: (0, 0)),
            pl.BlockSpec((S, LANES), lambda b: (0, 0)),
            pl.BlockSpec((None,) + st_block, lambda b: (b, l, 0, 0, 0)),
            pl.BlockSpec((None,) + st_block, lambda b: (b, l, 0, 0, 0)),
        ]
        args += [cos, sin, states[0], states[1]]
    else:
        for _ in range(2):
            out_shape.append(jax.ShapeDtypeStruct((B, RET_HEADS, RET_QK, RET_VD), F32))
            out_specs.append(pl.BlockSpec(st_block, lambda b: (b, 0, 0, 0)))
    return pl.pallas_call(
        functools.partial(_ret_kernel, latent=latent, S=S, l=l),
        grid=(B,),
        in_specs=in_specs,
        out_specs=out_specs,
        out_shape=out_shape,
        scratch_shapes=[pltpu.VMEM((S, QW), F32), pltpu.VMEM((S, QW), F32),
                        pltpu.VMEM((S, VW), F32), pltpu.VMEM((S, VW), F32),
                        pltpu.VMEM((2 * RET_HEADS, LANES, RET_VD), F32),
                        pltpu.VMEM((2 * RET_HEADS, RET_CHUNK, RET_CHUNK), F32),
                        pltpu.VMEM((2 * RET_HEADS, RET_CHUNK, LANES), F32),
                        pltpu.VMEM((2 * RET_HEADS, RET_CHUNK, LANES), F32)],
        compiler_params=_cp(("arbitrary",)),
        name="retention_lat" if latent else "retention_ctx",
    )(*args)


def _merge_kernel(yd_ref, ym_ref, yr_ref, g0_ref, g1_ref, g2_ref, wd_ref, wm_ref, wr_ref, o_ref):
    acc = _sigmoid(g0_ref[...]) * _dot(yd_ref[...], wd_ref[...])
    acc = acc + _sigmoid(g1_ref[...]) * _dot(ym_ref[...], wm_ref[...])
    acc = acc + _sigmoid(g2_ref[...]) * _dot(yr_ref[...], wr_ref[...])
    o_ref[...] = acc.astype(BF16)


def _merge_call(l, yd, ym, yr, proj, wd, wm, wr, tm):
    T = yd.shape[0]
    D = wd.shape[-1]

    def rows(w):
        return pl.BlockSpec((tm, w), lambda i: (i, 0))

    def gate(k):
        return pl.BlockSpec((tm, D), lambda i: (i, k))

    def wt(w):
        return pl.BlockSpec((None, w.shape[1], D), lambda i: (l, 0, 0))

    return pl.pallas_call(
        _merge_kernel,
        grid=(T // tm,),
        in_specs=[rows(yd.shape[1]), rows(ym.shape[1]), rows(yr.shape[1]),
                  gate(0), gate(1), gate(2), wt(wd), wt(wm), wt(wr)],
        out_specs=pl.BlockSpec((tm, D), lambda i: (i, 0)),
        out_shape=jax.ShapeDtypeStruct((T, D), BF16),
        compiler_params=_cp(("arbitrary",)),
        name="branch_merge",
    )(yd, ym, yr, proj, proj, proj, wd, wm, wr)


def _post_kernel(m_ref, x_ref, wo_ref, g1_ref, sc_ref, sh_ref, n2_ref, wr_ref, br_ref,
                 x1_ref, h2_ref, route_ref, cnt_ref, run_ref, *, E, tm):
    @pl.when(pl.program_id(0) == 0)
    def _():
        run_ref[...] = jnp.zeros_like(run_ref)

    x1 = x_ref[...] + g1_ref[...] * _dot(m_ref[...], wo_ref[...])
    x1_ref[...] = x1
    h = _rms(x1) * n2_ref[...]
    h = h * (1.0 + sc_ref[...]) + sh_ref[...]
    h2_ref[...] = h
    w = wr_ref[...]
    h_hi, w_hi = h.astype(BF16), w.astype(BF16)
    h_lo = (h - h_hi.astype(F32)).astype(BF16)
    w_lo = (w - w_hi.astype(F32)).astype(BF16)
    logits = _dot(h_hi, w_hi) + (_dot(h_hi, w_lo) + _dot(h_lo, w_hi)) + br_ref[...]
    lane = _lane_iota(logits.shape)
    cur = jnp.where(lane < E, logits, -jnp.inf)
    hots, vals = [], []
    for _ in range(TOP_K):
        m = jnp.max(cur, axis=-1, keepdims=True)
        idx = jnp.min(jnp.where(cur == m, lane, LANES), axis=-1, keepdims=True)
        hot = lane == idx
        hots.append(hot)
        vals.append(m)
        cur = jnp.where(hot, -jnp.inf, cur)
    exps = [jnp.exp(v - vals[0]) for v in vals]
    den = exps[0] + exps[1] + exps[2] + exps[3]
    sel = jnp.zeros(logits.shape, F32)
    for hot in hots:
        sel = sel + jnp.where(hot, 1.0, 0.0)
    ri = lax.broadcasted_iota(jnp.int32, (tm, tm), 0)
    ci = lax.broadcasted_iota(jnp.int32, (tm, tm), 1)
    tri = jnp.where(ci < ri, 1.0, 0.0).astype(BF16)
    rank = _dot(tri, sel.astype(BF16)) + run_ref[0:1, :]
    lane_f = lane.astype(F32)
    route = jnp.zeros(logits.shape, F32)
    for k in range(TOP_K):
        e_k = jnp.sum(jnp.where(hots[k], lane_f, 0.0), axis=-1, keepdims=True)
        r_k = jnp.sum(jnp.where(hots[k], rank, 0.0), axis=-1, keepdims=True)
        route = jnp.where(lane == k, e_k, route)
        route = jnp.where(lane == TOP_K + k, r_k, route)
        route = jnp.where(lane == 2 * TOP_K + k, exps[k] / den, route)
    route_ref[...] = route
    total = run_ref[...] + jnp.sum(sel, axis=0, keepdims=True)
    run_ref[...] = total
    cnt_ref[...] = total


def _post_call(l, merged, x, wo, modt, norm2, w_router, b_router, E, tm):
    T, D = x.shape

    def modc(k):
        return pl.BlockSpec((None, 1, D), lambda i: (i, 0, k))

    return pl.pallas_call(
        functools.partial(_post_kernel, E=E, tm=tm),
        grid=(T // tm,),
        in_specs=[
            pl.BlockSpec((tm, D), lambda i: (i, 0)),
            pl.BlockSpec((tm, D), lambda i: (i, 0)),
            pl.BlockSpec((None, D, D), lambda i: (l, 0, 0)),
            modc(2), modc(4), modc(3),
            pl.BlockSpec((None, 1, D), lambda i: (l, 0, 0)),
            pl.BlockSpec((None, D, LANES), lambda i: (l, 0, 0)),
            pl.BlockSpec((None, 1, LANES), lambda i: (l, 0, 0)),
        ],
        out_specs=[
            pl.BlockSpec((tm, D), lambda i: (i, 0)),
            pl.BlockSpec((tm, D), lambda i: (i, 0)),
            pl.BlockSpec((tm, LANES), lambda i: (i, 0)),
            pl.BlockSpec((COND_ROWS, LANES), lambda i: (0, 0)),
        ],
        out_shape=[
            jax.ShapeDtypeStruct((T, D), F32),
            jax.ShapeDtypeStruct((T, D), F32),
            jax.ShapeDtypeStruct((T, LANES), F32),
            jax.ShapeDtypeStruct((COND_ROWS, LANES), F32),
        ],
        scratch_shapes=[pltpu.VMEM((COND_ROWS, LANES), F32)],
        compiler_params=_cp(("arbitrary",)),
        name="out_proj_router",
    )(merged, x, wo, modt, modt, modt, norm2, w_router, b_router)


def _dispatch_kernel(dest_ref, pstart_ref, pcnt_ref, h_ref, xs_hbm, sem, *, tm, E):
    i = pl.program_id(0)

    def row_copy(r, dst):
        return pltpu.make_async_copy(h_ref.at[pl.ds(r, 1)], xs_hbm.at[pl.ds(dst, 1)], sem)

    def body(r, c):
        t = i * tm + r
        for k in range(TOP_K):
            row_copy(r, dest_ref[t * TOP_K + k]).start(priority=k % 2)
        return c

    lax.fori_loop(0, tm, body, 0, unroll=8)

    @pl.when(i == 0)
    def _():
        def pad_body(e, c):
            s = pstart_ref[e]
            n = pcnt_ref[e]
            lax.fori_loop(0, n, lambda r, c2: (row_copy(0, s + r).start(), c2)[1], 0)
            lax.fori_loop(0, n, lambda r, c2: (row_copy(0, 0).wait(), c2)[1], 0)
            return c

        lax.fori_loop(0, E, pad_body, 0)

    for _ in range(TOP_K):
        pltpu.make_async_copy(h_ref, xs_hbm.at[pl.ds(0, tm)], sem).wait()


def _dispatch_call(h2, dest, pstart, pcnt, rows, E, tm):
    T, D = h2.shape
    return pl.pallas_call(
        functools.partial(_dispatch_kernel, tm=tm, E=E),
        grid_spec=pltpu.PrefetchScalarGridSpec(
            num_scalar_prefetch=3,
            grid=(T // tm,),
            in_specs=[pl.BlockSpec((tm, D), lambda i, d, ps, pc: (i, 0))],
            out_specs=pl.BlockSpec(memory_space=pl.ANY),
            scratch_shapes=[pltpu.SemaphoreType.DMA],
        ),
        out_shape=jax.ShapeDtypeStruct((rows, D), F32),
        compiler_params=_cp(("arbitrary",)),
        name="expert_dispatch",
    )(dest, pstart, pcnt, h2)


def _expert_kernel(ge_ref, gs_ref, gn_ref, xs_hbm, wg_ref, wu_ref, wd_ref, bg_ref, bu_ref, bd_ref,
                   os_hbm, xland, xb, ybuf, wgb, wub, wdb, sem_in, sem_out, *, NF, G):
    g = pl.program_id(0)
    f = pl.program_id(1)
    n = gn_ref[g]
    start = gs_ref[g]
    g_next = jnp.minimum(g + 1, G - 1)
    n_next = jnp.where(g + 1 < G, gn_ref[g_next], 0)
    start_next = gs_ref[g_next]
    g_prev = jnp.maximum(g - 1, 0)
    n_prev = jnp.where(g > 0, gn_ref[g_prev], 0)
    start_prev = gs_ref[g_prev]

    def chunk_rows(s, c):
        return pl.ds(pl.multiple_of(s + c * EXPERT_CHUNK, EXPERT_CHUNK), EXPERT_CHUNK)

    def x_copy(s, c):
        return pltpu.make_async_copy(xs_hbm.at[chunk_rows(s, c)], xland.at[c], sem_in.at[c])

    def y_copy(s, c):
        return pltpu.make_async_copy(ybuf.at[c], os_hbm.at[chunk_rows(s, c)], sem_out.at[c])

    def for_chunks(cnt, fn, first=0):
        for c in range(first, GROUP_CHUNKS):
            pl.when(c < cnt)(functools.partial(fn, c))

    def load_chunk(c):
        x_copy(start, c).wait()
        xb[c] = xland[c].astype(BF16)

    def init_chunk(c):
        ybuf[c] = jnp.broadcast_to(bd_ref[...], ybuf.shape[1:])

    def wait_prev_rows():
        for_chunks(n_prev, lambda c: y_copy(start_prev, c).wait())

    def compute_chunk(c, cast_weights=False):
        x = xb[c]
        if cast_weights:
            wgb[...] = wg_ref[...].astype(BF16)
        gt = jnp.minimum(_dot(x, wgb[...]) + bg_ref[...], SWIGLU_LIMIT)
        if cast_weights:
            wub[...] = wu_ref[...].astype(BF16)
        up = jnp.clip(_dot(x, wub[...]) + bu_ref[...], -SWIGLU_LIMIT, SWIGLU_LIMIT)
        act = ((up + 1.0) * gt * _sigmoid(SWIGLU_ALPHA * gt)).astype(BF16)
        if cast_weights:
            wdb[...] = wd_ref[...].astype(BF16)
        ybuf[c] += _dot(act, wdb[...])

    @pl.when((f == 0) & (g == 0))
    def _():
        for_chunks(n, lambda c: x_copy(start, c).start())

    @pl.when((f == 0) & (n == 0))
    def _():
        wait_prev_rows()

    @pl.when(n > 0)
    def _():
        @pl.when(f == 0)
        def _():
            for_chunks(n, load_chunk)
            wait_prev_rows()
            for_chunks(n, init_chunk)

        @pl.when(f == NF - 1)
        def _():
            for_chunks(n_next, lambda c: x_copy(start_next, c).start())

        compute_chunk(0, cast_weights=True)
        for_chunks(n, compute_chunk, first=1)

        @pl.when(f == NF - 1)
        def _():
            for_chunks(n, lambda c: y_copy(start, c).start())

            @pl.when(g == G - 1)
            def _():
                for_chunks(n, lambda c: y_copy(start, c).wait())


def _expert_call(l, xs, ge, gs, gn, w_gate, w_up, w_down, b_gate, b_up, b_down, tf):
    rows, D = xs.shape
    F = w_gate.shape[-1]
    NF = F // tf
    G = ge.shape[0]

    def fsel(f, gn, g):
        return jnp.where(gn[g] > 0, f, NF - 1)

    return pl.pallas_call(
        functools.partial(_expert_kernel, NF=NF, G=G),
        grid_spec=pltpu.PrefetchScalarGridSpec(
            num_scalar_prefetch=3,
            grid=(G, NF),
            in_specs=[
                pl.BlockSpec(memory_space=pl.ANY),
                pl.BlockSpec((None, None, D, tf), lambda g, f, ge, gs, gn: (l, ge[g], 0, fsel(f, gn, g))),
                pl.BlockSpec((None, None, D, tf), lambda g, f, ge, gs, gn: (l, ge[g], 0, fsel(f, gn, g))),
                pl.BlockSpec((None, None, tf, D), lambda g, f, ge, gs, gn: (l, ge[g], fsel(f, gn, g), 0)),
                pl.BlockSpec((None, None, 1, tf), lambda g, f, ge, gs, gn: (l, ge[g], 0, fsel(f, gn, g))),
                pl.BlockSpec((None, None, 1, tf), lambda g, f, ge, gs, gn: (l, ge[g], 0, fsel(f, gn, g))),
                pl.BlockSpec((None, None, 1, D), lambda g, f, ge, gs, gn: (l, ge[g], 0, 0)),
            ],
            out_specs=pl.BlockSpec(memory_space=pl.ANY),
            scratch_shapes=[
                pltpu.VMEM((GROUP_CHUNKS, EXPERT_CHUNK, D), F32),
                pltpu.VMEM((GROUP_CHUNKS, EXPERT_CHUNK, D), BF16),
                pltpu.VMEM((GROUP_CHUNKS, EXPERT_CHUNK, D), F32),
                pltpu.VMEM((D, tf), BF16),
                pltpu.VMEM((D, tf), BF16),
                pltpu.VMEM((tf, D), BF16),
                pltpu.SemaphoreType.DMA((GROUP_CHUNKS,)),
                pltpu.SemaphoreType.DMA((GROUP_CHUNKS,)),
            ],
        ),
        out_shape=jax.ShapeDtypeStruct((rows, D), F32),
        compiler_params=_cp(("arbitrary", "arbitrary")),
        name="grouped_experts",
    )(ge, gs, gn, xs, w_gate, w_up, w_down, b_gate, b_up, b_down)


def _combine_kernel(dest_ref, os_hbm, x1_ref, g2_ref, route_ref, o_ref, buf, sem, *, tm):
    i = pl.program_id(0)
    nt = pl.num_programs(0)

    def issue(tile, slot):
        def body(r, c):
            t = tile * tm + r
            for k in range(TOP_K):
                pltpu.make_async_copy(os_hbm.at[pl.ds(dest_ref[t * TOP_K + k], 1)],
                                      buf.at[slot, k, pl.ds(r, 1)], sem.at[slot]).start(priority=k % 2)
            return c

        lax.fori_loop(0, tm, body, 0, unroll=8)

    @pl.when(i == 0)
    def _():
        issue(0, 0)

    @pl.when(i + 1 < nt)
    def _():
        issue(i + 1, (i + 1) % 2)

    slot = i % 2
    for k in range(TOP_K):
        pltpu.make_async_copy(os_hbm.at[pl.ds(0, tm)], buf.at[slot, k], sem.at[slot]).wait()
    route = route_ref[...]
    y = route[:, 2 * TOP_K:2 * TOP_K + 1] * buf[slot, 0]
    for k in range(1, TOP_K):
        y = y + route[:, 2 * TOP_K + k:2 * TOP_K + k + 1] * buf[slot, k]
    o_ref[...] = x1_ref[...] + g2_ref[...] * y


def _combine_call(dest, os_, x1, modt, route, tm):
    T, D = x1.shape
    return pl.pallas_call(
        functools.partial(_combine_kernel, tm=tm),
        grid_spec=pltpu.PrefetchScalarGridSpec(
            num_scalar_prefetch=1,
            grid=(T // tm,),
            in_specs=[
                pl.BlockSpec(memory_space=pl.ANY),
                pl.BlockSpec((tm, D), lambda i, d: (i, 0)),
                pl.BlockSpec((None, 1, D), lambda i, d: (i, 0, 5)),
                pl.BlockSpec((tm, LANES), lambda i, d: (i, 0)),
            ],
            out_specs=pl.BlockSpec((tm, D), lambda i, d: (i, 0)),
            scratch_shapes=[pltpu.VMEM((2, TOP_K, tm, D), F32), pltpu.SemaphoreType.DMA((2,))],
        ),
        out_shape=jax.ShapeDtypeStruct((T, D), F32),
        compiler_params=_cp(("arbitrary",)),
        name="expert_combine",
    )(dest, os_, x1, modt, route)


def _rope_tables(pos_a, pos_b, half, reps):
    freqs = ROPE_BASE ** (-jnp.arange(half, dtype=F32) / half)

    def blk(pos):
        ang = pos.astype(F32)[:, None] * freqs[None, :]
        c, s = jnp.cos(ang), jnp.sin(ang)
        return jnp.concatenate([c, c], -1), jnp.concatenate([-s, s], -1)

    ca, sa = blk(pos_a)
    cb, sb = blk(pos_b)
    cos = jnp.tile(jnp.concatenate([ca, cb], -1), (1, reps))
    sin = jnp.tile(jnp.concatenate([sa, sb], -1), (1, reps))
    return cos, sin


def _cond_tiles(T, Tp, Ss, tm):
    starts = np.arange(0, T, tm)
    return np.where(starts < Tp, 0, 1 + np.maximum(starts - Tp, 0) // Ss).astype(np.int32)


def _pick_tile(limit, *sizes):
    t = ROW_ALIGN
    while t * 2 <= limit and all(s % (t * 2) == 0 for s in sizes):
        t *= 2
    return t


def _route_plan(route, cnt, E, G):
    idx = route[:, 0:TOP_K].astype(jnp.int32)
    rank = route[:, TOP_K:2 * TOP_K].astype(jnp.int32)
    counts = cnt[0, :E].astype(jnp.int32)
    nch = (counts + EXPERT_CHUNK - 1) // EXPERT_CHUNK
    padded = nch * EXPERT_CHUNK
    base = jnp.cumsum(padded) - padded
    dest = (base[idx] + rank).reshape(-1)
    ngr = (nch + GROUP_CHUNKS - 1) // GROUP_CHUNKS
    gend = jnp.cumsum(ngr)
    gid = jnp.arange(G, dtype=jnp.int32)
    valid = gid < gend[-1]
    ge = jnp.minimum(jnp.sum((gid[:, None] >= gend[None, :]).astype(jnp.int32), axis=1), E - 1)
    j = gid - (gend - ngr)[ge]
    gs = base[ge] + j * (GROUP_CHUNKS * EXPERT_CHUNK)
    gn = jnp.clip(nch[ge] - j * GROUP_CHUNKS, 0, GROUP_CHUNKS)
    last = jnp.maximum(gend[-1] - 1, 0)
    ge = jnp.where(valid, ge, ge[last])
    gs = jnp.where(valid, gs, 0)
    gn = jnp.where(valid, gn, 0)
    return (dest.astype(jnp.int32), (base + counts).astype(jnp.int32), (padded - counts).astype(jnp.int32),
            ge.astype(jnp.int32), gs.astype(jnp.int32), gn.astype(jnp.int32))


def kernel(x_prompt, x_sample, cache_diff_k, cache_diff_v, cache_mla_ckv, cache_mla_krope, state_ret_fwd, state_ret_bwd, c, c_ctx, w_mod, b_mod, norm1, norm2, w_in, diff_q_gain, diff_k_gain, diff_lambda, diff_subln, mla_q_gain, w_uq, mla_kv_gain, w_ukv, mla_qn_gain, mla_kn_gain, ret_decay, w_br_diff, w_br_mla, w_br_ret, w_o, w_router, b_router, w_gate, b_gate, w_up, b_up, w_down, b_down):
    Bp, Sp, D = x_prompt.shape
    Bs, Ss, _ = x_sample.shape
    L = w_mod.shape[0]
    P = cache_diff_k.shape[2]
    E = w_router.shape[-1]
    F = w_gate.shape[-1]
    Tp, Ts = Bp * Sp, Bs * Ss
    T = Tp + Ts
    assert 1 + Bs <= COND_ROWS and E <= LANES
    assert Tp % ROW_ALIGN == 0 and Ss % ROW_ALIGN == 0 and Tp % Ss == 0 and Sp % RET_CHUNK == 0
    assert (3 * D) % 512 == 0 and F % 512 == 0

    off = 3 * D
    o_gl = sum((512, 512, 512, 512, 256, 64, 256, 256, 512, 512, 512))
    w_in_b = w_in.astype(BF16)
    w_in_p = jnp.concatenate([
        w_in_b[..., o_gl:], w_in_b[..., 0:2304], w_in_b[..., 2368:2880], w_in_b[..., 2304:2368],
        jnp.zeros((L, D, 192), BF16), w_in_b[..., 2880:o_gl]], axis=-1)
    assert w_in_p.shape[-1] == off + _SEC_TOTAL
    cond = jnp.zeros((COND_ROWS, D), F32).at[0].set(c_ctx).at[1:1 + Bs].set(c)
    wuq_p = jnp.pad(w_uq, ((0, 0), (0, 0), (0, 0), (0, MLA_NOPE + MLA_VD - MLA_QK))).reshape(
        L, MLA_Q_RANK, -1).astype(BF16)
    wukv_p = w_ukv.reshape(L, MLA_KV_RANK, -1).astype(BF16)
    pad_qk = ((0, 0), (0, MLA_NOPE + MLA_VD - MLA_QK))
    mla_w = (wuq_p, wukv_p, mla_q_gain[:, None, :], mla_kv_gain[:, None, :],
             jnp.pad(mla_qn_gain, pad_qk)[:, None, :], jnp.pad(mla_kn_gain, pad_qk)[:, None, :])
    diff_g = (jnp.tile(diff_q_gain, (1, 2))[:, None, :], jnp.tile(diff_k_gain, (1, 2))[:, None, :],
              diff_subln[:, None, :])
    wd_b, wm_b, wr_b, wo_b = (w.astype(BF16) for w in (w_br_diff, w_br_mla, w_br_ret, w_o))
    w_router_p = jnp.pad(w_router, ((0, 0), (0, 0), (0, LANES - E)))
    b_router_p = jnp.pad(b_router, ((0, 0), (0, LANES - E)))[:, None, :]
    lam_vec = diff_lambda.astype(F32)
    lam_init = jnp.asarray([0.8 - 0.6 * math.exp(-0.3 * l) for l in range(L)], F32)
    lam = (jnp.exp(jnp.sum(lam_vec[:, 0] * lam_vec[:, 1], -1))
           - jnp.exp(jnp.sum(lam_vec[:, 2] * lam_vec[:, 3], -1)) + lam_init)
    log_g = jax.nn.log_sigmoid(ret_decay.astype(F32)).reshape(-1)
    cdk = cache_diff_k.reshape(Bs, L, P, DIFF_HEADS * 2 * DIFF_HD)
    cdv = cache_diff_v.reshape(Bs, L, P, DIFF_HEADS * DIFF_VD)
    ckr = jnp.pad(cache_mla_krope, ((0, 0), (0, 0), (0, 0), (0, LANES - MLA_ROPE)))
    t_pos = jnp.arange(Ss, dtype=jnp.int32)
    tabs_ax = _rope_tables(t_pos // GRID_W, t_pos % GRID_W, 16, 2)
    tabs_1d = _rope_tables(t_pos, t_pos, 32, 1)
    bg4, bu4, bd4 = b_gate[:, :, None, :], b_up[:, :, None, :], b_down[:, :, None, :]

    tm_p = _pick_tile(1024, Tp, Ss)
    tm_r = ROW_ALIGN
    tq = ROW_ALIGN
    rows_sorted = T * TOP_K + E * EXPERT_CHUNK
    G = E + (T * TOP_K) // (GROUP_CHUNKS * EXPERT_CHUNK)
    tf = 512

    mod = _mod_call(cond, w_mod, b_mod)
    x = jnp.concatenate([x_prompt.reshape(Tp, D), x_sample.reshape(Ts, D)], axis=0)
    outs = [[] for _ in range(6)]
    for l in range(L):
        modt_p = mod[l][_cond_tiles(T, Tp, Ss, tm_p)][:, None, :]
        modt_r = mod[l][_cond_tiles(T, Tp, Ss, tm_r)][:, None, :]
        proj = _proj_call(l, x, modt_p, norm1[:, None, :], w_in_p, tm_p, 512)

        yd_p, kn_p = _diff_call(l, proj, off, lam, diff_g, Bp, Sp, 0, min(tq, Sp), False)
        (yd_s,) = _diff_call(l, proj, off, lam, diff_g, Bs, Ss, Tp, tq, True, (cdk, cdv), tabs_ax)
        ym_p, ckv_p = _mla_call(l, proj, off, mla_w, Bp, Sp, 0, min(tq, Sp), False)
        (ym_s,) = _mla_call(l, proj, off, mla_w, Bs, Ss, Tp, tq, True, (cache_mla_ckv, ckr), tabs_ax)
        yr_p, sf_p, sb_p = _ret_call(l, proj, off, log_g, Bp, Sp, 0, False)
        (yr_s,) = _ret_call(l, proj, off, log_g, Bs, Ss, Tp, True, (state_ret_fwd, state_ret_bwd), tabs_1d)

        yd = jnp.concatenate([yd_p, yd_s], axis=0)
        ym = jnp.concatenate([ym_p, ym_s], axis=0)
        yr = jnp.concatenate([yr_p, yr_s], axis=0)
        merged = _merge_call(l, yd, ym, yr, proj, wd_b, wm_b, wr_b, tm_r)
        x1, h2, route, cnt = _post_call(l, merged, x, wo_b, modt_r, norm2[:, None, :],
                                        w_router_p, b_router_p, E, tm_r)
        dest, pstart, pcnt, ge, gs, gn = _route_plan(route, cnt, E, G)
        xs = _dispatch_call(h2, dest, pstart, pcnt, rows_sorted, E, tm_r)
        os_ = _expert_call(l, xs, ge, gs, gn, w_gate, w_up, w_down, bg4, bu4, bd4, tf)
        x = _combine_call(dest, os_, x1, modt_r, route, tm_r)

        dv = proj[:Tp, off + _SEC["DV"]:off + _SEC["DV"] + DIFF_HEADS * DIFF_VD]
        kr = proj[:Tp, off + _SEC["MKR"]:off + _SEC["MKR"] + MLA_ROPE]
        for lst, v in zip(outs, (kn_p.reshape(Bp, Sp, DIFF_HEADS, 2 * DIFF_HD),
                                 dv.reshape(Bp, Sp, DIFF_HEADS, DIFF_VD),
                                 ckv_p.reshape(Bp, Sp, MLA_KV_RANK), kr.reshape(Bp, Sp, MLA_ROPE),
                                 sf_p, sb_p)):
            lst.append(v)

    y_prompt = x[:Tp].reshape(Bp, Sp, D)
    y_sample = x[Tp:].reshape(Bs, Ss, D)
    return (y_prompt, y_sample) + tuple(jnp.stack(o, axis=1) for o in outs)
```

```python
import functools
import math

import jax
import jax.numpy as jnp
import numpy as np
from jax import lax
from jax.experimental import pallas as pl
from jax.experimental.pallas import tpu as pltpu

F32 = jnp.float32
BF16 = jnp.bfloat16

GRID_W = 64
DIFF_HEADS = 4
DIFF_HD = 64
DIFF_VD = 128
MLA_HEADS = 8
MLA_Q_RANK = 512
MLA_KV_RANK = 256
MLA_NOPE = 128
MLA_ROPE = 64
MLA_VD = 128
MLA_QK = MLA_NOPE + MLA_ROPE
RET_HEADS = 4
RET_QK = 64
RET_VD = 128
RET_CHUNK = 128
TOP_K = 4
SWIGLU_LIMIT = 7.0
SWIGLU_ALPHA = 1.702
ROPE_BASE = 10000.0
EPS = 1e-6

LANES = 128
ROW_ALIGN = 256
EXPERT_CHUNK = 256
GROUP_CHUNKS = 4
COND_ROWS = 8
NORM_ROWS = 32
VMEM_LIMIT = 56 * 1024 * 1024

_SEC = dict(DQ=0, DK=512, DV=1024, MQ=1536, MKV=2048, RQ=2304, RK=2560, MKR=2816,
            RV=3072, RGF=3584, RGB=4096)
_SEC_TOTAL = 4608


def _cp(sem, vmem=VMEM_LIMIT):
    return pltpu.CompilerParams(dimension_semantics=sem, vmem_limit_bytes=vmem)


def _lane_iota(shape):
    return lax.broadcasted_iota(jnp.int32, shape, len(shape) - 1)


def _rms(x, n=None):
    n = x.shape[-1] if n is None else n
    return x * lax.rsqrt(jnp.sum(x * x, axis=-1, keepdims=True) / n + EPS)


def _dot(a, b):
    return jnp.dot(a, b, preferred_element_type=F32)


def _dot_nt(a, b):
    return lax.dot_general(a, b, (((1,), (1,)), ((), ())), preferred_element_type=F32)


def _softmax(s):
    e = jnp.exp(s - jnp.max(s, axis=-1, keepdims=True))
    return e / jnp.sum(e, axis=-1, keepdims=True)


def _sigmoid(x):
    return 1.0 / (1.0 + jnp.exp(-x))


_HI16 = 0xFFFF0000


def _pack_bf16_pairs(x):
    bits = lax.bitcast_convert_type(x.astype(BF16).astype(F32), jnp.uint32)
    half = x.shape[-1] // 2
    return (bits[:, :half] >> 16) | (bits[:, half:] & jnp.uint32(_HI16))


def _unpack_bf16_pairs(w):
    lo = lax.bitcast_convert_type(w << 16, F32).astype(BF16)
    hi = lax.bitcast_convert_type(w & jnp.uint32(_HI16), F32).astype(BF16)
    return lo, hi


def _rope(x, cos, sin, half):
    lane = _lane_iota(x.shape)
    up = pltpu.roll(x, LANES - half, 1)
    dn = pltpu.roll(x, half, 1)
    sw = jnp.where(lane % (2 * half) < half, up, dn)
    return x * cos + sw * sin


def _norm_halves(x, gain):
    lane = _lane_iota(x.shape)
    lo = lane < DIFF_HD
    x2 = x * x
    ss_lo = jnp.sum(jnp.where(lo, x2, 0.0), axis=-1, keepdims=True)
    ss_hi = jnp.sum(jnp.where(lo, 0.0, x2), axis=-1, keepdims=True)
    inv = jnp.where(lo, lax.rsqrt(ss_lo / DIFF_HD + EPS), lax.rsqrt(ss_hi / DIFF_HD + EPS))
    return x * inv * gain


def _mod_kernel(c_ref, w_ref, b_ref, o_ref):
    c = c_ref[...]
    a = (c * _sigmoid(c)).astype(BF16)
    o_ref[...] = _dot(a, w_ref[...].astype(BF16)) + b_ref[...]


def _mod_call(cond, w_mod, b_mod):
    L, D, N = w_mod.shape
    tn = 1024 if N % 1024 == 0 else 512
    return pl.pallas_call(
        _mod_kernel,
        grid=(L, N // tn),
        in_specs=[
            pl.BlockSpec((COND_ROWS, D), lambda l, j: (0, 0)),
            pl.BlockSpec((None, D, tn), lambda l, j: (l, 0, j)),
            pl.BlockSpec((None, 1, tn), lambda l, j: (l, 0, j)),
        ],
        out_specs=pl.BlockSpec((None, COND_ROWS, tn), lambda l, j: (l, 0, j)),
        out_shape=jax.ShapeDtypeStruct((L, COND_ROWS, N), F32),
        compiler_params=_cp(("arbitrary", "arbitrary")),
        name="adaln_mod",
    )(cond, w_mod, b_mod.reshape(L, 1, N))


def _proj_kernel(x_ref, sc_ref, sh_ref, g_ref, w_ref, o_ref, h_ref):
    @pl.when(pl.program_id(1) == 0)
    def _():
        def body(r, c):
            rows = pl.ds(pl.multiple_of(r * NORM_ROWS, NORM_ROWS), NORM_ROWS)
            h = _rms(x_ref[rows, :]) * g_ref[...]
            h_ref[rows, :] = (h * (1.0 + sc_ref[...]) + sh_ref[...]).astype(BF16)
            return c

        lax.fori_loop(0, x_ref.shape[0] // NORM_ROWS, body, 0)

    o_ref[...] = _dot(h_ref[...], w_ref[...])


def _proj_call(l, x, modt, norm1, w_in_p, tm, tn):
    T, D = x.shape
    PC = w_in_p.shape[-1]
    return pl.pallas_call(
        _proj_kernel,
        grid=(T // tm, PC // tn),
        in_specs=[
            pl.BlockSpec((tm, D), lambda i, j: (i, 0)),
            pl.BlockSpec((None, 1, D), lambda i, j: (i, 0, 1)),
            pl.BlockSpec((None, 1, D), lambda i, j: (i, 0, 0)),
            pl.BlockSpec((None, 1, D), lambda i, j: (l, 0, 0)),
            pl.BlockSpec((None, D, tn), lambda i, j: (l, 0, j)),
        ],
        out_specs=pl.BlockSpec((tm, tn), lambda i, j: (i, j)),
        out_shape=jax.ShapeDtypeStruct((T, PC), F32),
        scratch_shapes=[pltpu.VMEM((tm, D), BF16)],
        compiler_params=_cp(("arbitrary", "arbitrary")),
        name="norm_proj",
    )(x, modt, modt, norm1, w_in_p)


def _diff_kernel(*refs, latent, S, P, lam_init, l):
    if latent:
        (lam_ref, q_ref, k_ref, v_ref, qg_ref, kg_ref, sg_ref, ck_ref, cv_ref,
         cosq_ref, sinq_ref, cosk_ref, sink_ref, y_ref, kall, vall) = refs
    else:
        (lam_ref, q_ref, k_ref, v_ref, qg_ref, kg_ref, sg_ref, y_ref, kn_ref, kall, vall) = refs

    @pl.when(pl.program_id(1) == 0)
    def _():
        for h in range(DIFF_HEADS):
            sl = slice(LANES * h, LANES * (h + 1))
            kn = _norm_halves(k_ref[:, sl], kg_ref[...])
            if latent:
                kn = _rope(kn, cosk_ref[...], sink_ref[...], 16)
            else:
                kn_ref[:, sl] = kn
            kall[0:S, sl] = kn.astype(BF16)
        vall[0:S, :] = v_ref[...].astype(BF16)
        if latent:
            kall[S:S + P, :] = ck_ref[...].astype(BF16)
            vall[S:S + P, :] = cv_ref[...].astype(BF16)

    lam = lam_ref[l]
    scale = DIFF_HD ** -0.5
    for h in range(DIFF_HEADS):
        sl = slice(LANES * h, LANES * (h + 1))
        qn = _norm_halves(q_ref[:, sl], qg_ref[...])
        if latent:
            qn = _rope(qn, cosq_ref[...], sinq_ref[...], 16)
        lo = _lane_iota(qn.shape) < DIFF_HD
        q1 = jnp.where(lo, qn, 0.0).astype(BF16)
        q2 = jnp.where(lo, 0.0, qn).astype(BF16)
        kh = kall[:, sl]
        p = _softmax(_dot_nt(q1, kh) * scale) - lam * _softmax(_dot_nt(q2, kh) * scale)
        o = _dot(p.astype(BF16), vall[:, sl])
        y_ref[:, sl] = (_rms(o) * sg_ref[...] * (1.0 - lam_init)).astype(BF16)


def _diff_call(l, proj, off, lam, gains, B, S, row0, tq, latent, cache=None, tabs=None):
    qg, kg, sg = gains
    W = DIFF_HEADS * LANES
    nq = S // tq
    P = cache[0].shape[2] if latent else 0
    lam_init = 0.8 - 0.6 * math.exp(-0.3 * l)
    cq, ck, cv = ((off + _SEC[n]) // W for n in ("DQ", "DK", "DV"))
    in_specs = [
        pl.BlockSpec(memory_space=pltpu.SMEM),
        pl.BlockSpec((tq, W), lambda b, i: (row0 // tq + b * nq + i, cq)),
        pl.BlockSpec((S, W), lambda b, i: (row0 // S + b, ck)),
        pl.BlockSpec((S, W), lambda b, i: (row0 // S + b, cv)),
        pl.BlockSpec((None, 1, LANES), lambda b, i: (l, 0, 0)),
        pl.BlockSpec((None, 1, LANES), lambda b, i: (l, 0, 0)),
        pl.BlockSpec((None, 1, LANES), lambda b, i: (l, 0, 0)),
    ]
    args = [lam, proj, proj, proj, qg, kg, sg]
    out_shape = [jax.ShapeDtypeStruct((B * S, W), BF16)]
    out_specs = [pl.BlockSpec((tq, W), lambda b, i: (b * nq + i, 0))]
    if latent:
        cos, sin = tabs
        in_specs += [
            pl.BlockSpec((None, None, P, W), lambda b, i: (b, l, 0, 0)),
            pl.BlockSpec((None, None, P, W), lambda b, i: (b, l, 0, 0)),
            pl.BlockSpec((tq, LANES), lambda b, i: (i, 0)),
            pl.BlockSpec((tq, LANES), lambda b, i: (i, 0)),
            pl.BlockSpec((S, LANES), lambda b, i: (0, 0)),
            pl.BlockSpec((S, LANES), lambda b, i: (0, 0)),
        ]
        args += [cache[0], cache[1], cos, sin, cos, sin]
    else:
        out_shape.append(jax.ShapeDtypeStruct((B * S, W), F32))
        out_specs.append(pl.BlockSpec((S, W), lambda b, i: (b, 0)))
    return pl.pallas_call(
        functools.partial(_diff_kernel, latent=latent, S=S, P=P, lam_init=lam_init, l=l),
        grid=(B, nq),
        in_specs=in_specs,
        out_specs=out_specs,
        out_shape=out_shape,
        scratch_shapes=[pltpu.VMEM((S + P, W), BF16), pltpu.VMEM((S + P, W), BF16)],
        compiler_params=_cp(("arbitrary", "arbitrary")),
        name="diff_attn_lat" if latent else "diff_attn_ctx",
    )(*args)


def _mla_kernel(*refs, latent, S, P):
    if latent:
        (mq_ref, mkv_ref, mkr_ref, wuq_ref, wukv_ref, qg_ref, kvg_ref, qng_ref, kng_ref,
         cckv_ref, ckr_ref, cosq_ref, sinq_ref, cosk_ref, sink_ref, y_ref, kall, vall) = refs
    else:
        (mq_ref, mkv_ref, mkr_ref, wuq_ref, wukv_ref, qg_ref, kvg_ref, qng_ref, kng_ref,
         y_ref, ckv_ref, kall, vall) = refs
    HW = MLA_NOPE + MLA_VD

    @pl.when(pl.program_id(1) == 0)
    def _():
        ckv = _rms(mkv_ref[...]) * kvg_ref[...]
        if not latent:
            ckv_ref[...] = ckv
        segs = [(0, S, ckv, mkr_ref[...], latent)]
        if latent:
            segs.append((S, P, cckv_ref[...], ckr_ref[...], False))
        for r0, R, cv, kr, rot in segs:
            kv = _dot(cv.astype(BF16), wukv_ref[...])
            kr_ss = jnp.sum(kr * kr, axis=-1, keepdims=True)
            krg = kr * kng_ref[:, MLA_NOPE:]
            if rot:
                krg = _rope(krg, cosk_ref[...], sink_ref[...], 16)
            for h in range(MLA_HEADS):
                kn = kv[:, HW * h:HW * h + MLA_NOPE]
                ss = jnp.sum(kn * kn, axis=-1, keepdims=True) + kr_ss
                r = lax.rsqrt(ss / MLA_QK + EPS)
                kall[h, r0:r0 + R, 0:MLA_NOPE] = (kn * r * kng_ref[:, 0:MLA_NOPE]).astype(BF16)
                kall[h, r0:r0 + R, MLA_NOPE:] = (krg * r).astype(BF16)
                vall[h, r0:r0 + R, :] = kv[:, HW * h + MLA_NOPE:HW * (h + 1)].astype(BF16)

    cq = (_rms(mq_ref[...]) * qg_ref[...]).astype(BF16)
    qm = _dot(cq, wuq_ref[...])
    scale = MLA_QK ** -0.5
    for h in range(MLA_HEADS):
        qn = _rms(qm[:, HW * h:HW * (h + 1)], MLA_QK) * qng_ref[...]
        if latent:
            qr = _rope(qn[:, MLA_NOPE:], cosq_ref[...], sinq_ref[...], 16)
            qn = jnp.concatenate([qn[:, 0:MLA_NOPE], qr], axis=-1)
        p = _softmax(_dot_nt(qn.astype(BF16), kall[h]) * scale)
        y_ref[:, MLA_VD * h:MLA_VD * (h + 1)] = _dot(p.astype(BF16), vall[h]).astype(BF16)


def _mla_call(l, proj, off, weights, B, S, row0, tq, latent, cache=None, tabs=None):
    wuq, wukv, qg, kvg, qng, kng = weights
    nq = S // tq
    P = cache[0].shape[2] if latent else 0
    HW = MLA_NOPE + MLA_VD
    NW = MLA_HEADS * HW
    c_mq = (off + _SEC["MQ"]) // MLA_Q_RANK
    c_mkv = (off + _SEC["MKV"]) // MLA_KV_RANK
    c_mkr = (off + _SEC["MKR"]) // LANES

    def lay(shape):
        return pl.BlockSpec((None,) + shape, lambda b, i: (l,) + (0,) * len(shape))

    in_specs = [
        pl.BlockSpec((tq, MLA_Q_RANK), lambda b, i: (row0 // tq + b * nq + i, c_mq)),
        pl.BlockSpec((S, MLA_KV_RANK), lambda b, i: (row0 // S + b, c_mkv)),
        pl.BlockSpec((S, LANES), lambda b, i: (row0 // S + b, c_mkr)),
        lay((MLA_Q_RANK, NW)), lay((MLA_KV_RANK, NW)),
        lay((1, MLA_Q_RANK)), lay((1, MLA_KV_RANK)), lay((1, HW)), lay((1, HW)),
    ]
    args = [proj, proj, proj, wuq, wukv, qg, kvg, qng, kng]
    out_shape = [jax.ShapeDtypeStruct((B * S, MLA_HEADS * MLA_VD), BF16)]
    out_specs = [pl.BlockSpec((tq, MLA_HEADS * MLA_VD), lambda b, i: (b * nq + i, 0))]
    if latent:
        cos, sin = tabs
        in_specs += [
            pl.BlockSpec((None, None, P, MLA_KV_RANK), lambda b, i: (b, l, 0, 0)),
            pl.BlockSpec((None, None, P, LANES), lambda b, i: (b, l, 0, 0)),
            pl.BlockSpec((tq, LANES), lambda b, i: (i, 0)),
            pl.BlockSpec((tq, LANES), lambda b, i: (i, 0)),
            pl.BlockSpec((S, LANES), lambda b, i: (0, 0)),
            pl.BlockSpec((S, LANES), lambda b, i: (0, 0)),
        ]
        args += [cache[0], cache[1], cos, sin, cos, sin]
    else:
        out_shape.append(jax.ShapeDtypeStruct((B * S, MLA_KV_RANK), F32))
        out_specs.append(pl.BlockSpec((S, MLA_KV_RANK), lambda b, i: (b, 0)))
    return pl.pallas_call(
        functools.partial(_mla_kernel, latent=latent, S=S, P=P),
        grid=(B, nq),
        in_specs=in_specs,
        out_specs=out_specs,
        out_shape=out_shape,
        scratch_shapes=[pltpu.VMEM((MLA_HEADS, S + P, HW), BF16),
                        pltpu.VMEM((MLA_HEADS, S + P, MLA_VD), BF16)],
        compiler_params=_cp(("arbitrary", "arbitrary")),
        name="mla_lat" if latent else "mla_ctx",
    )(*args)


def _ret_kernel(*refs, latent, S, l):
    if latent:
        (lg_ref, rq_ref, rk_ref, rv_ref, gf_ref, gb_ref, cos_ref, sin_ref, s0f_ref, s0b_ref,
         y_ref, qs, ks, of, ob, st, dm, dq, dk) = refs
    else:
        (lg_ref, rq_ref, rk_ref, rv_ref, gf_ref, gb_ref, y_ref, sf_ref, sb_ref,
         qs, ks, of, ob, st, dm, dq, dk) = refs
    C = RET_CHUNK
    H = RET_HEADS
    nc = S // C

    @pl.when(pl.program_id(0) == 0)
    def _():
        ii = lax.broadcasted_iota(jnp.int32, (C, C), 0).astype(F32)
        jj = lax.broadcasted_iota(jnp.int32, (C, C), 1).astype(F32)
        row = lax.broadcasted_iota(jnp.int32, (C, LANES), 0).astype(F32)
        for d in range(2):
            for h in range(H):
                lg = lg_ref[(l * 2 + d) * H + h]
                rel = ii - jj if d == 0 else jj - ii
                dm[d * H + h] = jnp.where(rel >= 0, jnp.exp(jnp.maximum(rel, 0.0) * lg), 0.0)
                dq[d * H + h] = jnp.exp(((row + 1.0) if d == 0 else (C - row)) * lg)
                dk[d * H + h] = jnp.exp(((C - 1.0 - row) if d == 0 else row) * lg)

    for s in range(H // 2):
        ssl = slice(LANES * s, LANES * (s + 1))
        qslab = rq_ref[:, ssl]
        kslab = rk_ref[:, ssl] * (RET_QK ** -0.5)
        if latent:
            qslab = _rope(qslab, cos_ref[...], sin_ref[...], 32)
            kslab = _rope(kslab, cos_ref[...], sin_ref[...], 32)
        qs[:, ssl] = qslab
        ks[:, ssl] = kslab
    for d in range(2):
        for h in range(H):
            if latent:
                s0 = (s0f_ref if d == 0 else s0b_ref)[h]
                z = jnp.zeros_like(s0)
                st[d * H + h] = jnp.concatenate([s0, z] if h % 2 == 0 else [z, s0], axis=0)
            else:
                st[d * H + h] = jnp.zeros((LANES, RET_VD), F32)

    lo = _lane_iota((C, LANES)) < RET_QK
    one = jnp.ones((1, 1), F32)

    def body(n, carry):
        for d in range(2):
            r0 = pl.multiple_of((n if d == 0 else nc - 1 - n) * C, C)
            for h in range(H):
                i = d * H + h
                ssl = slice(LANES * (h // 2), LANES * (h // 2 + 1))
                vsl = slice(RET_VD * h, RET_VD * (h + 1))
                keep = lo if h % 2 == 0 else jnp.logical_not(lo)
                qc = jnp.where(keep, qs[pl.ds(r0, C), ssl], 0.0)
                kc = jnp.where(keep, ks[pl.ds(r0, C), ssl], 0.0)
                vc = rv_ref[pl.ds(r0, C), vsl].astype(BF16)
                sc = _dot_nt(qc.astype(BF16), kc.astype(BF16)) * dm[i]
                stv = st[i]
                o = _dot(sc.astype(BF16), vc) + _dot((qc * dq[i]).astype(BF16), stv.astype(BF16))
                (of if d == 0 else ob)[pl.ds(r0, C), vsl] = o
                kt = jnp.transpose(kc * dk[i]).astype(BF16)
                cdec = jnp.exp(one * (C * lg_ref[(l * 2 + d) * H + h]))
                st[i] = stv * cdec + _dot(kt, vc)
        return carry

    lax.fori_loop(0, nc, body, 0)

    for h in range(H):
        vsl = slice(RET_VD * h, RET_VD * (h + 1))
        if not latent:
            sf_ref[h] = st[h, RET_QK * (h % 2):RET_QK * (h % 2 + 1), :]
            sb_ref[h] = st[H + h, RET_QK * (h % 2):RET_QK * (h % 2 + 1), :]
        gf = gf_ref[:, vsl]
        gb = gb_ref[:, vsl]
        y = _rms(of[:, vsl]) * (gf * _sigmoid(gf)) + _rms(ob[:, vsl]) * (gb * _sigmoid(gb))
        y_ref[:, vsl] = y.astype(BF16)


def _ret_call(l, proj, off, lg, B, S, row0, latent, states=None, tabs=None):
    QW = RET_HEADS * RET_QK
    VW = RET_HEADS * RET_VD
    c_rq, c_rk = (off + _SEC["RQ"]) // QW, (off + _SEC["RK"]) // QW
    c_rv, c_gf, c_gb = ((off + _SEC[n]) // VW for n in ("RV", "RGF", "RGB"))
    rb = row0 // S
    in_specs = [
        pl.BlockSpec(memory_space=pltpu.SMEM),
        pl.BlockSpec((S, QW), lambda b: (rb + b, c_rq)),
        pl.BlockSpec((S, QW), lambda b: (rb + b, c_rk)),
        pl.BlockSpec((S, VW), lambda b: (rb + b, c_rv)),
        pl.BlockSpec((S, VW), lambda b: (rb + b, c_gf)),
        pl.BlockSpec((S, VW), lambda b: (rb + b, c_gb)),
    ]
    args = [lg, proj, proj, proj, proj, proj]
    out_shape = [jax.ShapeDtypeStruct((B * S, VW), BF16)]
    out_specs = [pl.BlockSpec((S, VW), lambda b: (b, 0))]
    st_block = (None, RET_HEADS, RET_QK, RET_VD)
    if latent:
        cos, sin = tabs
        in_specs += [
            pl.BlockSpec((S, LANES), lambda b: (0, 0)),
            pl.BlockSpec((S, LANES), lambda b: (0, 0)),
            pl.BlockSpec((None,) + st_block, lambda b: (b, l, 0, 0, 0)),
            pl.BlockSpec((None,) + st_block, lambda b: (b, l, 0, 0, 0)),
        ]
        args += [cos, sin, states[0], states[1]]
    else:
        for _ in range(2):
            out_shape.append(jax.ShapeDtypeStruct((B, RET_HEADS, RET_QK, RET_VD), F32))
            out_specs.append(pl.BlockSpec(st_block, lambda b: (b, 0, 0, 0)))
    return pl.pallas_call(
        functools.partial(_ret_kernel, latent=latent, S=S, l=l),
        grid=(B,),
        in_specs=in_specs,
        out_specs=out_specs,
        out_shape=out_shape,
        scratch_shapes=[pltpu.VMEM((S, QW), F32), pltpu.VMEM((S, QW), F32),
                        pltpu.VMEM((S, VW), F32), pltpu.VMEM((S, VW), F32),
                        pltpu.VMEM((2 * RET_HEADS, LANES, RET_VD), F32),
                        pltpu.VMEM((2 * RET_HEADS, RET_CHUNK, RET_CHUNK), F32),
                        pltpu.VMEM((2 * RET_HEADS, RET_CHUNK, LANES), F32),
                        pltpu.VMEM((2 * RET_HEADS, RET_CHUNK, LANES), F32)],
        compiler_params=_cp(("arbitrary",)),
        name="retention_lat" if latent else "retention_ctx",
    )(*args)


def _merge_kernel(yd_ref, ym_ref, yr_ref, g0_ref, g1_ref, g2_ref, wd_ref, wm_ref, wr_ref, o_ref):
    acc = _sigmoid(g0_ref[...]) * _dot(yd_ref[...], wd_ref[...])
    acc = acc + _sigmoid(g1_ref[...]) * _dot(ym_ref[...], wm_ref[...])
    acc = acc + _sigmoid(g2_ref[...]) * _dot(yr_ref[...], wr_ref[...])
    o_ref[...] = acc.astype(BF16)


def _merge_call(l, yd, ym, yr, proj, wd, wm, wr, tm):
    T = yd.shape[0]
    D = wd.shape[-1]

    def rows(w):
        return pl.BlockSpec((tm, w), lambda i: (i, 0))

    def gate(k):
        return pl.BlockSpec((tm, D), lambda i: (i, k))

    def wt(w):
        return pl.BlockSpec((None, w.shape[1], D), lambda i: (l, 0, 0))

    return pl.pallas_call(
        _merge_kernel,
        grid=(T // tm,),
        in_specs=[rows(yd.shape[1]), rows(ym.shape[1]), rows(yr.shape[1]),
                  gate(0), gate(1), gate(2), wt(wd), wt(wm), wt(wr)],
        out_specs=pl.BlockSpec((tm, D), lambda i: (i, 0)),
        out_shape=jax.ShapeDtypeStruct((T, D), BF16),
        compiler_params=_cp(("arbitrary",)),
        name="branch_merge",
    )(yd, ym, yr, proj, proj, proj, wd, wm, wr)


def _post_kernel(m_ref, x_ref, wo_ref, g1_ref, sc_ref, sh_ref, n2_ref, wr_ref, br_ref,
                 x1_ref, h2_ref, route_ref, cnt_ref, run_ref, *, E, tm):
    @pl.when(pl.program_id(0) == 0)
    def _():
        run_ref[...] = jnp.zeros_like(run_ref)

    x1 = x_ref[...] + g1_ref[...] * _dot(m_ref[...], wo_ref[...])
    x1_ref[...] = x1
    h = _rms(x1) * n2_ref[...]
    h = h * (1.0 + sc_ref[...]) + sh_ref[...]
    h2_ref[...] = _pack_bf16_pairs(h)
    w = wr_ref[...]
    h_hi, w_hi = h.astype(BF16), w.astype(BF16)
    h_lo = (h - h_hi.astype(F32)).astype(BF16)
    w_lo = (w - w_hi.astype(F32)).astype(BF16)
    logits = _dot(h_hi, w_hi) + (_dot(h_hi, w_lo) + _dot(h_lo, w_hi)) + br_ref[...]
    lane = _lane_iota(logits.shape)
    cur = jnp.where(lane < E, logits, -jnp.inf)
    hots, vals = [], []
    for _ in range(TOP_K):
        m = jnp.max(cur, axis=-1, keepdims=True)
        idx = jnp.min(jnp.where(cur == m, lane, LANES), axis=-1, keepdims=True)
        hot = lane == idx
        hots.append(hot)
        vals.append(m)
        cur = jnp.where(hot, -jnp.inf, cur)
    exps = [jnp.exp(v - vals[0]) for v in vals]
    den = exps[0] + exps[1] + exps[2] + exps[3]
    sel = jnp.zeros(logits.shape, F32)
    for hot in hots:
        sel = sel + jnp.where(hot, 1.0, 0.0)
    ri = lax.broadcasted_iota(jnp.int32, (tm, tm), 0)
    ci = lax.broadcasted_iota(jnp.int32, (tm, tm), 1)
    tri = jnp.where(ci < ri, 1.0, 0.0).astype(BF16)
    rank = _dot(tri, sel.astype(BF16)) + run_ref[0:1, :]
    lane_f = lane.astype(F32)
    route = jnp.zeros(logits.shape, F32)
    for k in range(TOP_K):
        e_k = jnp.sum(jnp.where(hots[k], lane_f, 0.0), axis=-1, keepdims=True)
        r_k = jnp.sum(jnp.where(hots[k], rank, 0.0), axis=-1, keepdims=True)
        route = jnp.where(lane == k, e_k, route)
        route = jnp.where(lane == TOP_K + k, r_k, route)
        route = jnp.where(lane == 2 * TOP_K + k, exps[k] / den, route)
    route_ref[...] = route
    total = run_ref[...] + jnp.sum(sel, axis=0, keepdims=True)
    run_ref[...] = total
    cnt_ref[...] = total


def _post_call(l, merged, x, wo, modt, norm2, w_router, b_router, E, tm):
    T, D = x.shape

    def modc(k):
        return pl.BlockSpec((None, 1, D), lambda i: (i, 0, k))

    return pl.pallas_call(
        functools.partial(_post_kernel, E=E, tm=tm),
        grid=(T // tm,),
        in_specs=[
            pl.BlockSpec((tm, D), lambda i: (i, 0)),
            pl.BlockSpec((tm, D), lambda i: (i, 0)),
            pl.BlockSpec((None, D, D), lambda i: (l, 0, 0)),
            modc(2), modc(4), modc(3),
            pl.BlockSpec((None, 1, D), lambda i: (l, 0, 0)),
            pl.BlockSpec((None, D, LANES), lambda i: (l, 0, 0)),
            pl.BlockSpec((None, 1, LANES), lambda i: (l, 0, 0)),
        ],
        out_specs=[
            pl.BlockSpec((tm, D), lambda i: (i, 0)),
            pl.BlockSpec((tm, D // 2), lambda i: (i, 0)),
            pl.BlockSpec((tm, LANES), lambda i: (i, 0)),
            pl.BlockSpec((COND_ROWS, LANES), lambda i: (0, 0)),
        ],
        out_shape=[
            jax.ShapeDtypeStruct((T, D), F32),
            jax.ShapeDtypeStruct((T, D // 2), jnp.uint32),
            jax.ShapeDtypeStruct((T, LANES), F32),
            jax.ShapeDtypeStruct((COND_ROWS, LANES), F32),
        ],
        scratch_shapes=[pltpu.VMEM((COND_ROWS, LANES), F32)],
        compiler_params=_cp(("arbitrary",)),
        name="out_proj_router",
    )(merged, x, wo, modt, modt, modt, norm2, w_router, b_router)


def _dispatch_kernel(dest_ref, pstart_ref, pcnt_ref, h_ref, xs_hbm, sem, *, tm, E):
    i = pl.program_id(0)

    def row_copy(r, dst):
        return pltpu.make_async_copy(h_ref.at[pl.ds(r, 1)], xs_hbm.at[pl.ds(dst, 1)], sem)

    def body(r, c):
        t = i * tm + r
        for k in range(TOP_K):
            row_copy(r, dest_ref[t * TOP_K + k]).start(priority=k % 2)
        return c

    lax.fori_loop(0, tm, body, 0, unroll=8)

    @pl.when(i == 0)
    def _():
        def pad_body(e, c):
            s = pstart_ref[e]
            n = pcnt_ref[e]
            lax.fori_loop(0, n, lambda r, c2: (row_copy(0, s + r).start(), c2)[1], 0)
            lax.fori_loop(0, n, lambda r, c2: (row_copy(0, 0).wait(), c2)[1], 0)
            return c

        lax.fori_loop(0, E, pad_body, 0)

    for _ in range(TOP_K):
        pltpu.make_async_copy(h_ref, xs_hbm.at[pl.ds(0, tm)], sem).wait()


def _dispatch_call(h2, dest, pstart, pcnt, rows, E, tm):
    T, D = h2.shape
    return pl.pallas_call(
        functools.partial(_dispatch_kernel, tm=tm, E=E),
        grid_spec=pltpu.PrefetchScalarGridSpec(
            num_scalar_prefetch=3,
            grid=(T // tm,),
            in_specs=[pl.BlockSpec((tm, D), lambda i, d, ps, pc: (i, 0))],
            out_specs=pl.BlockSpec(memory_space=pl.ANY),
            scratch_shapes=[pltpu.SemaphoreType.DMA],
        ),
        out_shape=jax.ShapeDtypeStruct((rows, D), h2.dtype),
        compiler_params=_cp(("arbitrary",)),
        name="expert_dispatch",
    )(dest, pstart, pcnt, h2)


def _expert_kernel(ge_ref, gs_ref, gn_ref, xs_hbm, wg_ref, wu_ref, wd_ref, bg_ref, bu_ref, bd_ref,
                   os_hbm, xland, xb, ybuf, wgb, wub, wdb, sem_in, sem_out, *, NF, G):
    g = pl.program_id(0)
    f = pl.program_id(1)
    n = gn_ref[g]
    start = gs_ref[g]
    g_next = jnp.minimum(g + 1, G - 1)
    n_next = jnp.where(g + 1 < G, gn_ref[g_next], 0)
    start_next = gs_ref[g_next]
    g_prev = jnp.maximum(g - 1, 0)
    n_prev = jnp.where(g > 0, gn_ref[g_prev], 0)
    start_prev = gs_ref[g_prev]

    def chunk_rows(s, c):
        return pl.ds(pl.multiple_of(s + c * EXPERT_CHUNK, EXPERT_CHUNK), EXPERT_CHUNK)

    def x_copy(s, c):
        return pltpu.make_async_copy(xs_hbm.at[chunk_rows(s, c)], xland.at[c], sem_in.at[c])

    def y_copy(s, c):
        return pltpu.make_async_copy(ybuf.at[c], os_hbm.at[chunk_rows(s, c)], sem_out.at[c])

    def for_chunks(cnt, fn, first=0):
        for c in range(first, GROUP_CHUNKS):
            pl.when(c < cnt)(functools.partial(fn, c))

    def load_chunk(c):
        x_copy(start, c).wait()
        half = xland.shape[-1]
        xb[c, :, 0:half], xb[c, :, half:] = _unpack_bf16_pairs(xland[c])

    def init_chunk(c):
        ybuf[c] = jnp.broadcast_to(bd_ref[...], ybuf.shape[1:])

    def wait_prev_rows():
        for_chunks(n_prev, lambda c: y_copy(start_prev, c).wait())

    def compute_chunk(c, cast_weights=False):
        x = xb[c]
        if cast_weights:
            wgb[...] = wg_ref[...].astype(BF16)
        gt = jnp.minimum(_dot(x, wgb[...]) + bg_ref[...], SWIGLU_LIMIT)
        if cast_weights:
            wub[...] = wu_ref[...].astype(BF16)
        up = jnp.clip(_dot(x, wub[...]) + bu_ref[...], -SWIGLU_LIMIT, SWIGLU_LIMIT)
        act = ((up + 1.0) * gt * _sigmoid(SWIGLU_ALPHA * gt)).astype(BF16)
        if cast_weights:
            wdb[...] = wd_ref[...].astype(BF16)
        ybuf[c] += _dot(act, wdb[...])

    @pl.when((f == 0) & (g == 0))
    def _():
        for_chunks(n, lambda c: x_copy(start, c).start())

    @pl.when((f == 0) & (n == 0))
    def _():
        wait_prev_rows()

    @pl.when(n > 0)
    def _():
        @pl.when(f == 0)
        def _():
            for_chunks(n, load_chunk)
            wait_prev_rows()
            for_chunks(n, init_chunk)

        @pl.when(f == NF - 1)
        def _():
            for_chunks(n_next, lambda c: x_copy(start_next, c).start())

        compute_chunk(0, cast_weights=True)
        for_chunks(n, compute_chunk, first=1)

        @pl.when(f == NF - 1)
        def _():
            for_chunks(n, lambda c: y_copy(start, c).start())

            @pl.when(g == G - 1)
            def _():
                for_chunks(n, lambda c: y_copy(start, c).wait())


def _expert_call(l, xs, ge, gs, gn, w_gate, w_up, w_down, b_gate, b_up, b_down, tf):
    rows = xs.shape[0]
    D, F = w_gate.shape[-2:]
    NF = F // tf
    G = ge.shape[0]

    def fsel(f, gn, g):
        return jnp.where(gn[g] > 0, f, NF - 1)

    return pl.pallas_call(
        functools.partial(_expert_kernel, NF=NF, G=G),
        grid_spec=pltpu.PrefetchScalarGridSpec(
            num_scalar_prefetch=3,
            grid=(G, NF),
            in_specs=[
                pl.BlockSpec(memory_space=pl.ANY),
                pl.BlockSpec((None, None, D, tf), lambda g, f, ge, gs, gn: (l, ge[g], 0, fsel(f, gn, g))),
                pl.BlockSpec((None, None, D, tf), lambda g, f, ge, gs, gn: (l, ge[g], 0, fsel(f, gn, g))),
                pl.BlockSpec((None, None, tf, D), lambda g, f, ge, gs, gn: (l, ge[g], fsel(f, gn, g), 0)),
                pl.BlockSpec((None, None, 1, tf), lambda g, f, ge, gs, gn: (l, ge[g], 0, fsel(f, gn, g))),
                pl.BlockSpec((None, None, 1, tf), lambda g, f, ge, gs, gn: (l, ge[g], 0, fsel(f, gn, g))),
                pl.BlockSpec((None, None, 1, D), lambda g, f, ge, gs, gn: (l, ge[g], 0, 0)),
            ],
            out_specs=pl.BlockSpec(memory_space=pl.ANY),
            scratch_shapes=[
                pltpu.VMEM((GROUP_CHUNKS, EXPERT_CHUNK, xs.shape[1]), xs.dtype),
                pltpu.VMEM((GROUP_CHUNKS, EXPERT_CHUNK, D), BF16),
                pltpu.VMEM((GROUP_CHUNKS, EXPERT_CHUNK, D), F32),
                pltpu.VMEM((D, tf), BF16),
                pltpu.VMEM((D, tf), BF16),
                pltpu.VMEM((tf, D), BF16),
                pltpu.SemaphoreType.DMA((GROUP_CHUNKS,)),
                pltpu.SemaphoreType.DMA((GROUP_CHUNKS,)),
            ],
        ),
        out_shape=jax.ShapeDtypeStruct((rows, D), F32),
        compiler_params=_cp(("arbitrary", "arbitrary")),
        name="grouped_experts",
    )(ge, gs, gn, xs, w_gate, w_up, w_down, b_gate, b_up, b_down)


def _combine_kernel(dest_ref, os_hbm, x1_ref, g2_ref, route_ref, o_ref, buf, sem, *, tm):
    i = pl.program_id(0)
    nt = pl.num_programs(0)

    def issue(tile, slot):
        def body(r, c):
            t = tile * tm + r
            for k in range(TOP_K):
                pltpu.make_async_copy(os_hbm.at[pl.ds(dest_ref[t * TOP_K + k], 1)],
                                      buf.at[slot, k, pl.ds(r, 1)], sem.at[slot]).start(priority=k % 2)
            return c

        lax.fori_loop(0, tm, body, 0, unroll=8)

    @pl.when(i == 0)
    def _():
        issue(0, 0)

    @pl.when(i + 1 < nt)
    def _():
        issue(i + 1, (i + 1) % 2)

    slot = i % 2
    for k in range(TOP_K):
        pltpu.make_async_copy(os_hbm.at[pl.ds(0, tm)], buf.at[slot, k], sem.at[slot]).wait()
    route = route_ref[...]
    y = route[:, 2 * TOP_K:2 * TOP_K + 1] * buf[slot, 0]
    for k in range(1, TOP_K):
        y = y + route[:, 2 * TOP_K + k:2 * TOP_K + k + 1] * buf[slot, k]
    o_ref[...] = x1_ref[...] + g2_ref[...] * y


def _combine_call(dest, os_, x1, modt, route, tm):
    T, D = x1.shape
    return pl.pallas_call(
        functools.partial(_combine_kernel, tm=tm),
        grid_spec=pltpu.PrefetchScalarGridSpec(
            num_scalar_prefetch=1,
            grid=(T // tm,),
            in_specs=[
                pl.BlockSpec(memory_space=pl.ANY),
                pl.BlockSpec((tm, D), lambda i, d: (i, 0)),
                pl.BlockSpec((None, 1, D), lambda i, d: (i, 0, 5)),
                pl.BlockSpec((tm, LANES), lambda i, d: (i, 0)),
            ],
            out_specs=pl.BlockSpec((tm, D), lambda i, d: (i, 0)),
            scratch_shapes=[pltpu.VMEM((2, TOP_K, tm, D), F32), pltpu.SemaphoreType.DMA((2,))],
        ),
        out_shape=jax.ShapeDtypeStruct((T, D), F32),
        compiler_params=_cp(("arbitrary",)),
        name="expert_combine",
    )(dest, os_, x1, modt, route)


def _rope_tables(pos_a, pos_b, half, reps):
    freqs = ROPE_BASE ** (-jnp.arange(half, dtype=F32) / half)

    def blk(pos):
        ang = pos.astype(F32)[:, None] * freqs[None, :]
        c, s = jnp.cos(ang), jnp.sin(ang)
        return jnp.concatenate([c, c], -1), jnp.concatenate([-s, s], -1)

    ca, sa = blk(pos_a)
    cb, sb = blk(pos_b)
    cos = jnp.tile(jnp.concatenate([ca, cb], -1), (1, reps))
    sin = jnp.tile(jnp.concatenate([sa, sb], -1), (1, reps))
    return cos, sin


def _cond_tiles(T, Tp, Ss, tm):
    starts = np.arange(0, T, tm)
    return np.where(starts < Tp, 0, 1 + np.maximum(starts - Tp, 0) // Ss).astype(np.int32)


def _pick_tile(limit, *sizes):
    t = ROW_ALIGN
    while t * 2 <= limit and all(s % (t * 2) == 0 for s in sizes):
        t *= 2
    return t


def _route_plan(route, cnt, E, G):
    idx = route[:, 0:TOP_K].astype(jnp.int32)
    rank = route[:, TOP_K:2 * TOP_K].astype(jnp.int32)
    counts = cnt[0, :E].astype(jnp.int32)
    nch = (counts + EXPERT_CHUNK - 1) // EXPERT_CHUNK
    padded = nch * EXPERT_CHUNK
    base = jnp.cumsum(padded) - padded
    dest = (base[idx] + rank).reshape(-1)
    ngr = (nch + GROUP_CHUNKS - 1) // GROUP_CHUNKS
    gend = jnp.cumsum(ngr)
    gid = jnp.arange(G, dtype=jnp.int32)
    valid = gid < gend[-1]
    ge = jnp.minimum(jnp.sum((gid[:, None] >= gend[None, :]).astype(jnp.int32), axis=1), E - 1)
    j = gid - (gend - ngr)[ge]
    gs = base[ge] + j * (GROUP_CHUNKS * EXPERT_CHUNK)
    gn = jnp.clip(nch[ge] - j * GROUP_CHUNKS, 0, GROUP_CHUNKS)
    last = jnp.maximum(gend[-1] - 1, 0)
    ge = jnp.where(valid, ge, ge[last])
    gs = jnp.where(valid, gs, 0)
    gn = jnp.where(valid, gn, 0)
    return (dest.astype(jnp.int32), (base + counts).astype(jnp.int32), (padded - counts).astype(jnp.int32),
            ge.astype(jnp.int32), gs.astype(jnp.int32), gn.astype(jnp.int32))


def kernel(x_prompt, x_sample, cache_diff_k, cache_diff_v, cache_mla_ckv, cache_mla_krope, state_ret_fwd, state_ret_bwd, c, c_ctx, w_mod, b_mod, norm1, norm2, w_in, diff_q_gain, diff_k_gain, diff_lambda, diff_subln, mla_q_gain, w_uq, mla_kv_gain, w_ukv, mla_qn_gain, mla_kn_gain, ret_decay, w_br_diff, w_br_mla, w_br_ret, w_o, w_router, b_router, w_gate, b_gate, w_up, b_up, w_down, b_down):
    Bp, Sp, D = x_prompt.shape
    Bs, Ss, _ = x_sample.shape
    L = w_mod.shape[0]
    P = cache_diff_k.shape[2]
    E = w_router.shape[-1]
    F = w_gate.shape[-1]
    Tp, Ts = Bp * Sp, Bs * Ss
    T = Tp + Ts
    assert 1 + Bs <= COND_ROWS and E <= LANES
    assert Tp % ROW_ALIGN == 0 and Ss % ROW_ALIGN == 0 and Tp % Ss == 0 and Sp % RET_CHUNK == 0
    assert (3 * D) % 512 == 0 and F % 512 == 0

    off = 3 * D
    o_gl = sum((512, 512, 512, 512, 256, 64, 256, 256, 512, 512, 512))
    w_in_b = w_in.astype(BF16)
    w_in_p = jnp.concatenate([
        w_in_b[..., o_gl:], w_in_b[..., 0:2304], w_in_b[..., 2368:2880], w_in_b[..., 2304:2368],
        jnp.zeros((L, D, 192), BF16), w_in_b[..., 2880:o_gl]], axis=-1)
    assert w_in_p.shape[-1] == off + _SEC_TOTAL
    cond = jnp.zeros((COND_ROWS, D), F32).at[0].set(c_ctx).at[1:1 + Bs].set(c)
    wuq_p = jnp.pad(w_uq, ((0, 0), (0, 0), (0, 0), (0, MLA_NOPE + MLA_VD - MLA_QK))).reshape(
        L, MLA_Q_RANK, -1).astype(BF16)
    wukv_p = w_ukv.reshape(L, MLA_KV_RANK, -1).astype(BF16)
    pad_qk = ((0, 0), (0, MLA_NOPE + MLA_VD - MLA_QK))
    mla_w = (wuq_p, wukv_p, mla_q_gain[:, None, :], mla_kv_gain[:, None, :],
             jnp.pad(mla_qn_gain, pad_qk)[:, None, :], jnp.pad(mla_kn_gain, pad_qk)[:, None, :])
    diff_g = (jnp.tile(diff_q_gain, (1, 2))[:, None, :], jnp.tile(diff_k_gain, (1, 2))[:, None, :],
              diff_subln[:, None, :])
    wd_b, wm_b, wr_b, wo_b = (w.astype(BF16) for w in (w_br_diff, w_br_mla, w_br_ret, w_o))
    w_router_p = jnp.pad(w_router, ((0, 0), (0, 0), (0, LANES - E)))
    b_router_p = jnp.pad(b_router, ((0, 0), (0, LANES - E)))[:, None, :]
    lam_vec = diff_lambda.astype(F32)
    lam_init = jnp.asarray([0.8 - 0.6 * math.exp(-0.3 * l) for l in range(L)], F32)
    lam = (jnp.exp(jnp.sum(lam_vec[:, 0] * lam_vec[:, 1], -1))
           - jnp.exp(jnp.sum(lam_vec[:, 2] * lam_vec[:, 3], -1)) + lam_init)
    log_g = jax.nn.log_sigmoid(ret_decay.astype(F32)).reshape(-1)
    cdk = cache_diff_k.reshape(Bs, L, P, DIFF_HEADS * 2 * DIFF_HD)
    cdv = cache_diff_v.reshape(Bs, L, P, DIFF_HEADS * DIFF_VD)
    ckr = jnp.pad(cache_mla_krope, ((0, 0), (0, 0), (0, 0), (0, LANES - MLA_ROPE)))
    t_pos = jnp.arange(Ss, dtype=jnp.int32)
    tabs_ax = _rope_tables(t_pos // GRID_W, t_pos % GRID_W, 16, 2)
    tabs_1d = _rope_tables(t_pos, t_pos, 32, 1)
    bg4, bu4, bd4 = b_gate[:, :, None, :], b_up[:, :, None, :], b_down[:, :, None, :]

    tm_p = _pick_tile(1024, Tp, Ss)
    tm_r = ROW_ALIGN
    tq = ROW_ALIGN
    rows_sorted = T * TOP_K + E * EXPERT_CHUNK
    G = E + (T * TOP_K) // (GROUP_CHUNKS * EXPERT_CHUNK)
    tf = 512

    mod = _mod_call(cond, w_mod, b_mod)
    x = jnp.concatenate([x_prompt.reshape(Tp, D), x_sample.reshape(Ts, D)], axis=0)
    outs = [[] for _ in range(6)]
    for l in range(L):
        modt_p = mod[l][_cond_tiles(T, Tp, Ss, tm_p)][:, None, :]
        modt_r = mod[l][_cond_tiles(T, Tp, Ss, tm_r)][:, None, :]
        proj = _proj_call(l, x, modt_p, norm1[:, None, :], w_in_p, tm_p, 512)

        yd_p, kn_p = _diff_call(l, proj, off, lam, diff_g, Bp, Sp, 0, min(tq, Sp), False)
        (yd_s,) = _diff_call(l, proj, off, lam, diff_g, Bs, Ss, Tp, tq, True, (cdk, cdv), tabs_ax)
        ym_p, ckv_p = _mla_call(l, proj, off, mla_w, Bp, Sp, 0, min(tq, Sp), False)
        (ym_s,) = _mla_call(l, proj, off, mla_w, Bs, Ss, Tp, tq, True, (cache_mla_ckv, ckr), tabs_ax)
        yr_p, sf_p, sb_p = _ret_call(l, proj, off, log_g, Bp, Sp, 0, False)
        (yr_s,) = _ret_call(l, proj, off, log_g, Bs, Ss, Tp, True, (state_ret_fwd, state_ret_bwd), tabs_1d)

        yd = jnp.concatenate([yd_p, yd_s], axis=0)
        ym = jnp.concatenate([ym_p, ym_s], axis=0)
        yr = jnp.concatenate([yr_p, yr_s], axis=0)
        merged = _merge_call(l, yd, ym, yr, proj, wd_b, wm_b, wr_b, tm_r)
        x1, h2, route, cnt = _post_call(l, merged, x, wo_b, modt_r, norm2[:, None, :],
                                        w_router_p, b_router_p, E, tm_r)
        dest, pstart, pcnt, ge, gs, gn = _route_plan(route, cnt, E, G)
        xs = _dispatch_call(h2, dest, pstart, pcnt, rows_sorted, E, tm_r)
        os_ = _expert_call(l, xs, ge, gs, gn, w_gate, w_up, w_down, bg4, bu4, bd4, tf)
        x = _combine_call(dest, os_, x1, modt_r, route, tm_r)

        dv = proj[:Tp, off + _SEC["DV"]:off + _SEC["DV"] + DIFF_HEADS * DIFF_VD]
        kr = proj[:Tp, off + _SEC["MKR"]:off + _SEC["MKR"] + MLA_ROPE]
        for lst, v in zip(outs, (kn_p.reshape(Bp, Sp, DIFF_HEADS, 2 * DIFF_HD),
                                 dv.reshape(Bp, Sp, DIFF_HEADS, DIFF_VD),
                                 ckv_p.reshape(Bp, Sp, MLA_KV_RANK), kr.reshape(Bp, Sp, MLA_ROPE),
                                 sf_p, sb_p)):
            lst.append(v)

    y_prompt = x[:Tp].reshape(Bp, Sp, D)
    y_sample = x[Tp:].reshape(Bs, Ss, D)
    return (y_prompt, y_sample) + tuple(jnp.stack(o, axis=1) for o in outs)
```

```python
import functools
import math

import jax
import jax.numpy as jnp
import numpy as np
from jax import lax
from jax.experimental import pallas as pl
from jax.experimental.pallas import tpu as pltpu

F32 = jnp.float32
BF16 = jnp.bfloat16

GRID_W = 64
DIFF_HEADS = 4
DIFF_HD = 64
DIFF_VD = 128
MLA_HEADS = 8
MLA_Q_RANK = 512
MLA_KV_RANK = 256
MLA_NOPE = 128
MLA_ROPE = 64
MLA_VD = 128
MLA_QK = MLA_NOPE + MLA_ROPE
RET_HEADS = 4
RET_QK = 64
RET_VD = 128
RET_CHUNK = 128
TOP_K = 4
SWIGLU_LIMIT = 7.0
SWIGLU_ALPHA = 1.702
ROPE_BASE = 10000.0
EPS = 1e-6

LANES = 128
ROW_ALIGN = 256
EXPERT_CHUNK = 256
GROUP_CHUNKS = 4
COND_ROWS = 8
VMEM_LIMIT = 56 * 1024 * 1024

_SEC = dict(DQ=0, DK=512, DV=1024, MQ=1536, MKV=2048, RQ=2304, RK=2560, MKR=2816,
            RV=3072, RGF=3584, RGB=4096)
_SEC_TOTAL = 4608


def _cp(sem, vmem=VMEM_LIMIT):
    return pltpu.CompilerParams(dimension_semantics=sem, vmem_limit_bytes=vmem)


def _lane_iota(shape):
    return lax.broadcasted_iota(jnp.int32, shape, len(shape) - 1)


def _rms(x, n=None):
    n = x.shape[-1] if n is None else n
    return x * lax.rsqrt(jnp.sum(x * x, axis=-1, keepdims=True) / n + EPS)


def _dot(a, b):
    return jnp.dot(a, b, preferred_element_type=F32)


def _dot_nt(a, b):
    return lax.dot_general(a, b, (((1,), (1,)), ((), ())), preferred_element_type=F32)


def _softmax(s):
    e = jnp.exp(s - jnp.max(s, axis=-1, keepdims=True))
    return e / jnp.sum(e, axis=-1, keepdims=True)


def _sigmoid(x):
    return 1.0 / (1.0 + jnp.exp(-x))


_HI16 = 0xFFFF0000


def _pack_bf16_pairs(x):
    bits = lax.bitcast_convert_type(x.astype(BF16).astype(F32), jnp.uint32)
    half = x.shape[-1] // 2
    return (bits[:, :half] >> 16) | (bits[:, half:] & jnp.uint32(_HI16))


def _unpack_bf16_pairs(w):
    lo = lax.bitcast_convert_type(w << 16, F32).astype(BF16)
    hi = lax.bitcast_convert_type(w & jnp.uint32(_HI16), F32).astype(BF16)
    return lo, hi


def _rope(x, cos, sin, half):
    lane = _lane_iota(x.shape)
    up = pltpu.roll(x, LANES - half, 1)
    dn = pltpu.roll(x, half, 1)
    sw = jnp.where(lane % (2 * half) < half, up, dn)
    return x * cos + sw * sin


def _norm_halves(x, gain):
    lane = _lane_iota(x.shape)
    lo = lane < DIFF_HD
    x2 = x * x
    ss_lo = jnp.sum(jnp.where(lo, x2, 0.0), axis=-1, keepdims=True)
    ss_hi = jnp.sum(jnp.where(lo, 0.0, x2), axis=-1, keepdims=True)
    inv = jnp.where(lo, lax.rsqrt(ss_lo / DIFF_HD + EPS), lax.rsqrt(ss_hi / DIFF_HD + EPS))
    return x * inv * gain


def _mod_kernel(c_ref, w_ref, b_ref, o_ref):
    c = c_ref[...]
    a = (c * _sigmoid(c)).astype(BF16)
    o_ref[...] = _dot(a, w_ref[...].astype(BF16)) + b_ref[...]


def _mod_call(cond, w_mod, b_mod):
    L, D, N = w_mod.shape
    tn = 1024 if N % 1024 == 0 else 512
    return pl.pallas_call(
        _mod_kernel,
        grid=(L, N // tn),
        in_specs=[
            pl.BlockSpec((COND_ROWS, D), lambda l, j: (0, 0)),
            pl.BlockSpec((None, D, tn), lambda l, j: (l, 0, j)),
            pl.BlockSpec((None, 1, tn), lambda l, j: (l, 0, j)),
        ],
        out_specs=pl.BlockSpec((None, COND_ROWS, tn), lambda l, j: (l, 0, j)),
        out_shape=jax.ShapeDtypeStruct((L, COND_ROWS, N), F32),
        compiler_params=_cp(("arbitrary", "arbitrary")),
        name="adaln_mod",
    )(cond, w_mod, b_mod.reshape(L, 1, N))


def _proj_kernel(x_ref, sc_ref, sh_ref, g_ref, w_ref, o_ref, h_ref):
    @pl.when(pl.program_id(1) == 0)
    def _():
        h = _rms(x_ref[...]) * g_ref[...]
        h_ref[...] = (h * (1.0 + sc_ref[...]) + sh_ref[...]).astype(BF16)

    o_ref[...] = _dot(h_ref[...], w_ref[...])


def _proj_call(l, x, modt, norm1, w_in_p, tm, tn):
    T, D = x.shape
    PC = w_in_p.shape[-1]
    return pl.pallas_call(
        _proj_kernel,
        grid=(T // tm, PC // tn),
        in_specs=[
            pl.BlockSpec((tm, D), lambda i, j: (i, 0)),
            pl.BlockSpec((None, 1, D), lambda i, j: (i, 0, 1)),
            pl.BlockSpec((None, 1, D), lambda i, j: (i, 0, 0)),
            pl.BlockSpec((None, 1, D), lambda i, j: (l, 0, 0)),
            pl.BlockSpec((None, D, tn), lambda i, j: (l, 0, j)),
        ],
        out_specs=pl.BlockSpec((tm, tn), lambda i, j: (i, j)),
        out_shape=jax.ShapeDtypeStruct((T, PC), F32),
        scratch_shapes=[pltpu.VMEM((tm, D), BF16)],
        compiler_params=_cp(("arbitrary", "arbitrary")),
        name="norm_proj",
    )(x, modt, modt, norm1, w_in_p)


def _diff_kernel(*refs, latent, S, P, lam_init, l):
    if latent:
        (lam_ref, q_ref, k_ref, v_ref, qg_ref, kg_ref, sg_ref, ck_ref, cv_ref,
         cosq_ref, sinq_ref, cosk_ref, sink_ref, y_ref, kall, vall) = refs
    else:
        (lam_ref, q_ref, k_ref, v_ref, qg_ref, kg_ref, sg_ref, y_ref, kn_ref, kall, vall) = refs

    @pl.when(pl.program_id(1) == 0)
    def _():
        for h in range(DIFF_HEADS):
            sl = slice(LANES * h, LANES * (h + 1))
            kn = _norm_halves(k_ref[:, sl], kg_ref[...])
            if latent:
                kn = _rope(kn, cosk_ref[...], sink_ref[...], 16)
            else:
                kn_ref[:, sl] = kn
            kall[0:S, sl] = kn.astype(BF16)
        vall[0:S, :] = v_ref[...].astype(BF16)
        if latent:
            kall[S:S + P, :] = ck_ref[...].astype(BF16)
            vall[S:S + P, :] = cv_ref[...].astype(BF16)

    lam = lam_ref[l]
    scale = DIFF_HD ** -0.5
    for h in range(DIFF_HEADS):
        sl = slice(LANES * h, LANES * (h + 1))
        qn = _norm_halves(q_ref[:, sl], qg_ref[...])
        if latent:
            qn = _rope(qn, cosq_ref[...], sinq_ref[...], 16)
        lo = _lane_iota(qn.shape) < DIFF_HD
        q1 = jnp.where(lo, qn, 0.0).astype(BF16)
        q2 = jnp.where(lo, 0.0, qn).astype(BF16)
        kh = kall[:, sl]
        p = _softmax(_dot_nt(q1, kh) * scale) - lam * _softmax(_dot_nt(q2, kh) * scale)
        o = _dot(p.astype(BF16), vall[:, sl])
        y_ref[:, sl] = (_rms(o) * sg_ref[...] * (1.0 - lam_init)).astype(BF16)


def _diff_call(l, proj, off, lam, gains, B, S, row0, tq, latent, cache=None, tabs=None):
    qg, kg, sg = gains
    W = DIFF_HEADS * LANES
    nq = S // tq
    P = cache[0].shape[2] if latent else 0
    lam_init = 0.8 - 0.6 * math.exp(-0.3 * l)
    cq, ck, cv = ((off + _SEC[n]) // W for n in ("DQ", "DK", "DV"))
    in_specs = [
        pl.BlockSpec(memory_space=pltpu.SMEM),
        pl.BlockSpec((tq, W), lambda b, i: (row0 // tq + b * nq + i, cq)),
        pl.BlockSpec((S, W), lambda b, i: (row0 // S + b, ck)),
        pl.BlockSpec((S, W), lambda b, i: (row0 // S + b, cv)),
        pl.BlockSpec((None, 1, LANES), lambda b, i: (l, 0, 0)),
        pl.BlockSpec((None, 1, LANES), lambda b, i: (l, 0, 0)),
        pl.BlockSpec((None, 1, LANES), lambda b, i: (l, 0, 0)),
    ]
    args = [lam, proj, proj, proj, qg, kg, sg]
    out_shape = [jax.ShapeDtypeStruct((B * S, W), BF16)]
    out_specs = [pl.BlockSpec((tq, W), lambda b, i: (b * nq + i, 0))]
    if latent:
        cos, sin = tabs
        in_specs += [
            pl.BlockSpec((None, None, P, W), lambda b, i: (b, l, 0, 0)),
            pl.BlockSpec((None, None, P, W), lambda b, i: (b, l, 0, 0)),
            pl.BlockSpec((tq, LANES), lambda b, i: (i, 0)),
            pl.BlockSpec((tq, LANES), lambda b, i: (i, 0)),
            pl.BlockSpec((S, LANES), lambda b, i: (0, 0)),
            pl.BlockSpec((S, LANES), lambda b, i: (0, 0)),
        ]
        args += [cache[0], cache[1], cos, sin, cos, sin]
    else:
        out_shape.append(jax.ShapeDtypeStruct((B * S, W), F32))
        out_specs.append(pl.BlockSpec((S, W), lambda b, i: (b, 0)))
    return pl.pallas_call(
        functools.partial(_diff_kernel, latent=latent, S=S, P=P, lam_init=lam_init, l=l),
        grid=(B, nq),
        in_specs=in_specs,
        out_specs=out_specs,
        out_shape=out_shape,
        scratch_shapes=[pltpu.VMEM((S + P, W), BF16), pltpu.VMEM((S + P, W), BF16)],
        compiler_params=_cp(("arbitrary", "arbitrary")),
        name="diff_attn_lat" if latent else "diff_attn_ctx",
    )(*args)


def _mla_kernel(*refs, latent, S, P):
    if latent:
        (mq_ref, mkv_ref, mkr_ref, wuq_ref, wukv_ref, qg_ref, kvg_ref, qng_ref, kng_ref,
         cckv_ref, ckr_ref, cosq_ref, sinq_ref, cosk_ref, sink_ref, y_ref, kall, vall) = refs
    else:
        (mq_ref, mkv_ref, mkr_ref, wuq_ref, wukv_ref, qg_ref, kvg_ref, qng_ref, kng_ref,
         y_ref, ckv_ref, kall, vall) = refs
    HW = MLA_NOPE + MLA_VD

    @pl.when(pl.program_id(1) == 0)
    def _():
        ckv = _rms(mkv_ref[...]) * kvg_ref[...]
        if not latent:
            ckv_ref[...] = ckv
        segs = [(0, S, ckv, mkr_ref[...], latent)]
        if latent:
            segs.append((S, P, cckv_ref[...], ckr_ref[...], False))
        for r0, R, cv, kr, rot in segs:
            kv = _dot(cv.astype(BF16), wukv_ref[...])
            kr_ss = jnp.sum(kr * kr, axis=-1, keepdims=True)
            krg = kr * kng_ref[:, MLA_NOPE:]
            if rot:
                krg = _rope(krg, cosk_ref[...], sink_ref[...], 16)
            for h in range(MLA_HEADS):
                kn = kv[:, HW * h:HW * h + MLA_NOPE]
                ss = jnp.sum(kn * kn, axis=-1, keepdims=True) + kr_ss
                r = lax.rsqrt(ss / MLA_QK + EPS)
                kall[h, r0:r0 + R, 0:MLA_NOPE] = (kn * r * kng_ref[:, 0:MLA_NOPE]).astype(BF16)
                kall[h, r0:r0 + R, MLA_NOPE:] = (krg * r).astype(BF16)
                vall[h, r0:r0 + R, :] = kv[:, HW * h + MLA_NOPE:HW * (h + 1)].astype(BF16)

    cq = (_rms(mq_ref[...]) * qg_ref[...]).astype(BF16)
    qm = _dot(cq, wuq_ref[...])
    scale = MLA_QK ** -0.5
    for h in range(MLA_HEADS):
        qn = _rms(qm[:, HW * h:HW * (h + 1)], MLA_QK) * qng_ref[...]
        if latent:
            qr = _rope(qn[:, MLA_NOPE:], cosq_ref[...], sinq_ref[...], 16)
            qn = jnp.concatenate([qn[:, 0:MLA_NOPE], qr], axis=-1)
        p = _softmax(_dot_nt(qn.astype(BF16), kall[h]) * scale)
        y_ref[:, MLA_VD * h:MLA_VD * (h + 1)] = _dot(p.astype(BF16), vall[h]).astype(BF16)


def _mla_call(l, proj, off, weights, B, S, row0, tq, latent, cache=None, tabs=None):
    wuq, wukv, qg, kvg, qng, kng = weights
    nq = S // tq
    P = cache[0].shape[2] if latent else 0
    HW = MLA_NOPE + MLA_VD
    NW = MLA_HEADS * HW
    c_mq = (off + _SEC["MQ"]) // MLA_Q_RANK
    c_mkv = (off + _SEC["MKV"]) // MLA_KV_RANK
    c_mkr = (off + _SEC["MKR"]) // LANES

    def lay(shape):
        return pl.BlockSpec((None,) + shape, lambda b, i: (l,) + (0,) * len(shape))

    in_specs = [
        pl.BlockSpec((tq, MLA_Q_RANK), lambda b, i: (row0 // tq + b * nq + i, c_mq)),
        pl.BlockSpec((S, MLA_KV_RANK), lambda b, i: (row0 // S + b, c_mkv)),
        pl.BlockSpec((S, LANES), lambda b, i: (row0 // S + b, c_mkr)),
        lay((MLA_Q_RANK, NW)), lay((MLA_KV_RANK, NW)),
        lay((1, MLA_Q_RANK)), lay((1, MLA_KV_RANK)), lay((1, HW)), lay((1, HW)),
    ]
    args = [proj, proj, proj, wuq, wukv, qg, kvg, qng, kng]
    out_shape = [jax.ShapeDtypeStruct((B * S, MLA_HEADS * MLA_VD), BF16)]
    out_specs = [pl.BlockSpec((tq, MLA_HEADS * MLA_VD), lambda b, i: (b * nq + i, 0))]
    if latent:
        cos, sin = tabs
        in_specs += [
            pl.BlockSpec((None, None, P, MLA_KV_RANK), lambda b, i: (b, l, 0, 0)),
            pl.BlockSpec((None, None, P, LANES), lambda b, i: (b, l, 0, 0)),
            pl.BlockSpec((tq, LANES), lambda b, i: (i, 0)),
            pl.BlockSpec((tq, LANES), lambda b, i: (i, 0)),
            pl.BlockSpec((S, LANES), lambda b, i: (0, 0)),
            pl.BlockSpec((S, LANES), lambda b, i: (0, 0)),
        ]
        args += [cache[0], cache[1], cos, sin, cos, sin]
    else:
        out_shape.append(jax.ShapeDtypeStruct((B * S, MLA_KV_RANK), F32))
        out_specs.append(pl.BlockSpec((S, MLA_KV_RANK), lambda b, i: (b, 0)))
    return pl.pallas_call(
        functools.partial(_mla_kernel, latent=latent, S=S, P=P),
        grid=(B, nq),
        in_specs=in_specs,
        out_specs=out_specs,
        out_shape=out_shape,
        scratch_shapes=[pltpu.VMEM((MLA_HEADS, S + P, HW), BF16),
                        pltpu.VMEM((MLA_HEADS, S + P, MLA_VD), BF16)],
        compiler_params=_cp(("arbitrary", "arbitrary")),
        name="mla_lat" if latent else "mla_ctx",
    )(*args)


def _ret_kernel(*refs, latent, S, l):
    if latent:
        (lg_ref, rq_ref, rk_ref, rv_ref, gf_ref, gb_ref, cos_ref, sin_ref, s0f_ref, s0b_ref,
         y_ref, qs, ks, of, ob, st, dm, dq, dk) = refs
    else:
        (lg_ref, rq_ref, rk_ref, rv_ref, gf_ref, gb_ref, y_ref, sf_ref, sb_ref,
         qs, ks, of, ob, st, dm, dq, dk) = refs
    C = RET_CHUNK
    H = RET_HEADS
    nc = S // C

    @pl.when(pl.program_id(0) == 0)
    def _():
        ii = lax.broadcasted_iota(jnp.int32, (C, C), 0).astype(F32)
        jj = lax.broadcasted_iota(jnp.int32, (C, C), 1).astype(F32)
        row = lax.broadcasted_iota(jnp.int32, (C, LANES), 0).astype(F32)
        for d in range(2):
            for h in range(H):
                lg = lg_ref[(l * 2 + d) * H + h]
                rel = ii - jj if d == 0 else jj - ii
                dm[d * H + h] = jnp.where(rel >= 0, jnp.exp(jnp.maximum(rel, 0.0) * lg), 0.0)
                dq[d * H + h] = jnp.exp(((row + 1.0) if d == 0 else (C - row)) * lg)
                dk[d * H + h] = jnp.exp(((C - 1.0 - row) if d == 0 else row) * lg)

    for s in range(H // 2):
        ssl = slice(LANES * s, LANES * (s + 1))
        qslab = rq_ref[:, ssl]
        kslab = rk_ref[:, ssl] * (RET_QK ** -0.5)
        if latent:
            qslab = _rope(qslab, cos_ref[...], sin_ref[...], 32)
            kslab = _rope(kslab, cos_ref[...], sin_ref[...], 32)
        qs[:, ssl] = qslab
        ks[:, ssl] = kslab
    for d in range(2):
        for h in range(H):
            if latent:
                s0 = (s0f_ref if d == 0 else s0b_ref)[h]
                z = jnp.zeros_like(s0)
                st[d * H + h] = jnp.concatenate([s0, z] if h % 2 == 0 else [z, s0], axis=0)
            else:
                st[d * H + h] = jnp.zeros((LANES, RET_VD), F32)

    lo = _lane_iota((C, LANES)) < RET_QK
    one = jnp.ones((1, 1), F32)

    def body(n, carry):
        for d in range(2):
            r0 = pl.multiple_of((n if d == 0 else nc - 1 - n) * C, C)
            for h in range(H):
                i = d * H + h
                ssl = slice(LANES * (h // 2), LANES * (h // 2 + 1))
                vsl = slice(RET_VD * h, RET_VD * (h + 1))
                keep = lo if h % 2 == 0 else jnp.logical_not(lo)
                qc = jnp.where(keep, qs[pl.ds(r0, C), ssl], 0.0)
                kc = jnp.where(keep, ks[pl.ds(r0, C), ssl], 0.0)
                vc = rv_ref[pl.ds(r0, C), vsl].astype(BF16)
                sc = _dot_nt(qc.astype(BF16), kc.astype(BF16)) * dm[i]
                stv = st[i]
                o = _dot(sc.astype(BF16), vc) + _dot((qc * dq[i]).astype(BF16), stv.astype(BF16))
                (of if d == 0 else ob)[pl.ds(r0, C), vsl] = o
                kt = jnp.transpose(kc * dk[i]).astype(BF16)
                cdec = jnp.exp(one * (C * lg_ref[(l * 2 + d) * H + h]))
                st[i] = stv * cdec + _dot(kt, vc)
        return carry

    lax.fori_loop(0, nc, body, 0)

    for h in range(H):
        vsl = slice(RET_VD * h, RET_VD * (h + 1))
        if not latent:
            sf_ref[h] = st[h, RET_QK * (h % 2):RET_QK * (h % 2 + 1), :]
            sb_ref[h] = st[H + h, RET_QK * (h % 2):RET_QK * (h % 2 + 1), :]
        gf = gf_ref[:, vsl]
        gb = gb_ref[:, vsl]
        y = _rms(of[:, vsl]) * (gf * _sigmoid(gf)) + _rms(ob[:, vsl]) * (gb * _sigmoid(gb))
        y_ref[:, vsl] = y.astype(BF16)


def _ret_call(l, proj, off, lg, B, S, row0, latent, states=None, tabs=None):
    QW = RET_HEADS * RET_QK
    VW = RET_HEADS * RET_VD
    c_rq, c_rk = (off + _SEC["RQ"]) // QW, (off + _SEC["RK"]) // QW
    c_rv, c_gf, c_gb = ((off + _SEC[n]) // VW for n in ("RV", "RGF", "RGB"))
    rb = row0 // S
    in_specs = [
        pl.BlockSpec(memory_space=pltpu.SMEM),
        pl.BlockSpec((S, QW), lambda b: (rb + b, c_rq)),
        pl.BlockSpec((S, QW), lambda b: (rb + b, c_rk)),
        pl.BlockSpec((S, VW), lambda b: (rb + b, c_rv)),
        pl.BlockSpec((S, VW), lambda b: (rb + b, c_gf)),
        pl.BlockSpec((S, VW), lambda b: (rb + b, c_gb)),
    ]
    args = [lg, proj, proj, proj, proj, proj]
    out_shape = [jax.ShapeDtypeStruct((B * S, VW), BF16)]
    out_specs = [pl.BlockSpec((S, VW), lambda b: (b, 0))]
    st_block = (None, RET_HEADS, RET_QK, RET_VD)
    if latent:
        cos, sin = tabs
        in_specs += [
            pl.BlockSpec((S, LANES), lambda b: (0, 0)),
            pl.BlockSpec((S, LANES), lambda b: (0, 0)),
            pl.BlockSpec((None,) + st_block, lambda b: (b, l, 0, 0, 0)),
            pl.BlockSpec((None,) + st_block, lambda b: (b, l, 0, 0, 0)),
        ]
        args += [cos, sin, states[0], states[1]]
    else:
        for _ in range(2):
            out_shape.append(jax.ShapeDtypeStruct((B, RET_HEADS, RET_QK, RET_VD), F32))
            out_specs.append(pl.BlockSpec(st_block, lambda b: (b, 0, 0, 0)))
    return pl.pallas_call(
        functools.partial(_ret_kernel, latent=latent, S=S, l=l),
        grid=(B,),
        in_specs=in_specs,
        out_specs=out_specs,
        out_shape=out_shape,
        scratch_shapes=[pltpu.VMEM((S, QW), F32), pltpu.VMEM((S, QW), F32),
                        pltpu.VMEM((S, VW), F32), pltpu.VMEM((S, VW), F32),
                        pltpu.VMEM((2 * RET_HEADS, LANES, RET_VD), F32),
                        pltpu.VMEM((2 * RET_HEADS, RET_CHUNK, RET_CHUNK), F32),
                        pltpu.VMEM((2 * RET_HEADS, RET_CHUNK, LANES), F32),
                        pltpu.VMEM((2 * RET_HEADS, RET_CHUNK, LANES), F32)],
        compiler_params=_cp(("arbitrary",)),
        name="retention_lat" if latent else "retention_ctx",
    )(*args)


def _merge_kernel(yd_ref, ym_ref, yr_ref, g0_ref, g1_ref, g2_ref, wd_ref, wm_ref, wr_ref, o_ref):
    acc = _sigmoid(g0_ref[...]) * _dot(yd_ref[...], wd_ref[...])
    acc = acc + _sigmoid(g1_ref[...]) * _dot(ym_ref[...], wm_ref[...])
    acc = acc + _sigmoid(g2_ref[...]) * _dot(yr_ref[...], wr_ref[...])
    o_ref[...] = acc.astype(BF16)


def _merge_call(l, yd, ym, yr, proj, wd, wm, wr, tm):
    T = yd.shape[0]
    D = wd.shape[-1]

    def rows(w):
        return pl.BlockSpec((tm, w), lambda i: (i, 0))

    def gate(k):
        return pl.BlockSpec((tm, D), lambda i: (i, k))

    def wt(w):
        return pl.BlockSpec((None, w.shape[1], D), lambda i: (l, 0, 0))

    return pl.pallas_call(
        _merge_kernel,
        grid=(T // tm,),
        in_specs=[rows(yd.shape[1]), rows(ym.shape[1]), rows(yr.shape[1]),
                  gate(0), gate(1), gate(2), wt(wd), wt(wm), wt(wr)],
        out_specs=pl.BlockSpec((tm, D), lambda i: (i, 0)),
        out_shape=jax.ShapeDtypeStruct((T, D), BF16),
        compiler_params=_cp(("arbitrary",)),
        name="branch_merge",
    )(yd, ym, yr, proj, proj, proj, wd, wm, wr)


def _post_kernel(m_ref, x_ref, wo_ref, g1_ref, sc_ref, sh_ref, n2_ref, wr_ref, br_ref,
                 x1_ref, h2_ref, route_ref, cnt_ref, run_ref, *, E, tm):
    @pl.when(pl.program_id(0) == 0)
    def _():
        run_ref[...] = jnp.zeros_like(run_ref)

    x1 = x_ref[...] + g1_ref[...] * _dot(m_ref[...], wo_ref[...])
    x1_ref[...] = x1
    h = _rms(x1) * n2_ref[...]
    h = h * (1.0 + sc_ref[...]) + sh_ref[...]
    h2_ref[...] = _pack_bf16_pairs(h)
    w = wr_ref[...]
    h_hi, w_hi = h.astype(BF16), w.astype(BF16)
    h_lo = (h - h_hi.astype(F32)).astype(BF16)
    w_lo = (w - w_hi.astype(F32)).astype(BF16)
    logits = _dot(h_hi, w_hi) + (_dot(h_hi, w_lo) + _dot(h_lo, w_hi)) + br_ref[...]
    lane = _lane_iota(logits.shape)
    cur = jnp.where(lane < E, logits, -jnp.inf)
    hots, vals = [], []
    for _ in range(TOP_K):
        m = jnp.max(cur, axis=-1, keepdims=True)
        idx = jnp.min(jnp.where(cur == m, lane, LANES), axis=-1, keepdims=True)
        hot = lane == idx
        hots.append(hot)
        vals.append(m)
        cur = jnp.where(hot, -jnp.inf, cur)
    exps = [jnp.exp(v - vals[0]) for v in vals]
    den = exps[0] + exps[1] + exps[2] + exps[3]
    sel = jnp.zeros(logits.shape, F32)
    for hot in hots:
        sel = sel + jnp.where(hot, 1.0, 0.0)
    ri = lax.broadcasted_iota(jnp.int32, (tm, tm), 0)
    ci = lax.broadcasted_iota(jnp.int32, (tm, tm), 1)
    tri = jnp.where(ci < ri, 1.0, 0.0).astype(BF16)
    rank = _dot(tri, sel.astype(BF16)) + run_ref[0:1, :]
    lane_f = lane.astype(F32)
    route = jnp.zeros(logits.shape, F32)
    for k in range(TOP_K):
        e_k = jnp.sum(jnp.where(hots[k], lane_f, 0.0), axis=-1, keepdims=True)
        r_k = jnp.sum(jnp.where(hots[k], rank, 0.0), axis=-1, keepdims=True)
        route = jnp.where(lane == k, e_k, route)
        route = jnp.where(lane == TOP_K + k, r_k, route)
        route = jnp.where(lane == 2 * TOP_K + k, exps[k] / den, route)
    route_ref[...] = route
    total = run_ref[...] + jnp.sum(sel, axis=0, keepdims=True)
    run_ref[...] = total
    cnt_ref[...] = total


def _post_call(l, merged, x, wo, modt, norm2, w_router, b_router, E, tm):
    T, D = x.shape

    def modc(k):
        return pl.BlockSpec((None, 1, D), lambda i: (i, 0, k))

    return pl.pallas_call(
        functools.partial(_post_kernel, E=E, tm=tm),
        grid=(T // tm,),
        in_specs=[
            pl.BlockSpec((tm, D), lambda i: (i, 0)),
            pl.BlockSpec((tm, D), lambda i: (i, 0)),
            pl.BlockSpec((None, D, D), lambda i: (l, 0, 0)),
            modc(2), modc(4), modc(3),
            pl.BlockSpec((None, 1, D), lambda i: (l, 0, 0)),
            pl.BlockSpec((None, D, LANES), lambda i: (l, 0, 0)),
            pl.BlockSpec((None, 1, LANES), lambda i: (l, 0, 0)),
        ],
        out_specs=[
            pl.BlockSpec((tm, D), lambda i: (i, 0)),
            pl.BlockSpec((tm, D // 2), lambda i: (i, 0)),
            pl.BlockSpec((tm, LANES), lambda i: (i, 0)),
            pl.BlockSpec((COND_ROWS, LANES), lambda i: (0, 0)),
        ],
        out_shape=[
            jax.ShapeDtypeStruct((T, D), F32),
            jax.ShapeDtypeStruct((T, D // 2), jnp.uint32),
            jax.ShapeDtypeStruct((T, LANES), F32),
            jax.ShapeDtypeStruct((COND_ROWS, LANES), F32),
        ],
        scratch_shapes=[pltpu.VMEM((COND_ROWS, LANES), F32)],
        compiler_params=_cp(("arbitrary",)),
        name="out_proj_router",
    )(merged, x, wo, modt, modt, modt, norm2, w_router, b_router)


def _dispatch_kernel(dest_ref, pstart_ref, pcnt_ref, h_ref, xs_hbm, sem, *, tm, E):
    i = pl.program_id(0)

    def row_copy(r, dst):
        return pltpu.make_async_copy(h_ref.at[pl.ds(r, 1)], xs_hbm.at[pl.ds(dst, 1)], sem)

    def body(r, c):
        t = i * tm + r
        for k in range(TOP_K):
            row_copy(r, dest_ref[t * TOP_K + k]).start(priority=k % 2)
        return c

    lax.fori_loop(0, tm, body, 0, unroll=8)

    @pl.when(i == 0)
    def _():
        def pad_body(e, c):
            s = pstart_ref[e]
            n = pcnt_ref[e]
            lax.fori_loop(0, n, lambda r, c2: (row_copy(0, s + r).start(), c2)[1], 0)
            lax.fori_loop(0, n, lambda r, c2: (row_copy(0, 0).wait(), c2)[1], 0)
            return c

        lax.fori_loop(0, E, pad_body, 0)

    for _ in range(TOP_K):
        pltpu.make_async_copy(h_ref, xs_hbm.at[pl.ds(0, tm)], sem).wait()


def _dispatch_call(h2, dest, pstart, pcnt, rows, E, tm):
    T, D = h2.shape
    return pl.pallas_call(
        functools.partial(_dispatch_kernel, tm=tm, E=E),
        grid_spec=pltpu.PrefetchScalarGridSpec(
            num_scalar_prefetch=3,
            grid=(T // tm,),
            in_specs=[pl.BlockSpec((tm, D), lambda i, d, ps, pc: (i, 0))],
            out_specs=pl.BlockSpec(memory_space=pl.ANY),
            scratch_shapes=[pltpu.SemaphoreType.DMA],
        ),
        out_shape=jax.ShapeDtypeStruct((rows, D), h2.dtype),
        compiler_params=_cp(("arbitrary",)),
        name="expert_dispatch",
    )(dest, pstart, pcnt, h2)


def _expert_kernel(ge_ref, gs_ref, gn_ref, xs_hbm, wg_ref, wu_ref, wd_ref, bg_ref, bu_ref, bd_ref,
                   os_hbm, xland, xb, ybuf, wgb, wub, wdb, sem_in, sem_out, *, NF, G):
    g = pl.program_id(0)
    f = pl.program_id(1)
    n = gn_ref[g]
    start = gs_ref[g]
    g_next = jnp.minimum(g + 1, G - 1)
    n_next = jnp.where(g + 1 < G, gn_ref[g_next], 0)
    start_next = gs_ref[g_next]
    g_prev = jnp.maximum(g - 1, 0)
    n_prev = jnp.where(g > 0, gn_ref[g_prev], 0)
    start_prev = gs_ref[g_prev]

    def chunk_rows(s, c):
        return pl.ds(pl.multiple_of(s + c * EXPERT_CHUNK, EXPERT_CHUNK), EXPERT_CHUNK)

    def span(c, k=1):
        return slice(c * EXPERT_CHUNK, (c + k) * EXPERT_CHUNK)

    def x_copy(s, c):
        return pltpu.make_async_copy(xs_hbm.at[chunk_rows(s, c)], xland.at[c], sem_in.at[c])

    def y_copy(s, c):
        return pltpu.make_async_copy(ybuf.at[span(c)], os_hbm.at[chunk_rows(s, c)], sem_out.at[c])

    def for_chunks(cnt, fn, first=0):
        for c in range(first, GROUP_CHUNKS):
            pl.when(c < cnt)(functools.partial(fn, c))

    def load_chunk(c):
        x_copy(start, c).wait()
        half = xland.shape[-1]
        xb[span(c), 0:half], xb[span(c), half:] = _unpack_bf16_pairs(xland[c])

    def init_chunk(c):
        ybuf[span(c), :] = jnp.broadcast_to(bd_ref[...], (EXPERT_CHUNK, ybuf.shape[1]))

    def wait_prev_rows():
        for_chunks(n_prev, lambda c: y_copy(start_prev, c).wait())

    def compute_span(c, k, cast_weights=False):
        x = xb[span(c, k), :]
        if cast_weights:
            wgb[...] = wg_ref[...].astype(BF16)
        gt = jnp.minimum(_dot(x, wgb[...]) + bg_ref[...], SWIGLU_LIMIT)
        if cast_weights:
            wub[...] = wu_ref[...].astype(BF16)
        up = jnp.clip(_dot(x, wub[...]) + bu_ref[...], -SWIGLU_LIMIT, SWIGLU_LIMIT)
        act = ((up + 1.0) * gt * _sigmoid(SWIGLU_ALPHA * gt)).astype(BF16)
        if cast_weights:
            wdb[...] = wd_ref[...].astype(BF16)
        ybuf[span(c, k), :] += _dot(act, wdb[...])

    @pl.when((f == 0) & (g == 0))
    def _():
        for_chunks(n, lambda c: x_copy(start, c).start())

    @pl.when((f == 0) & (n == 0))
    def _():
        wait_prev_rows()

    @pl.when(n > 0)
    def _():
        @pl.when(f == 0)
        def _():
            for_chunks(n, load_chunk)
            wait_prev_rows()
            for_chunks(n, init_chunk)

        @pl.when(f == NF - 1)
        def _():
            for_chunks(n_next, lambda c: x_copy(start_next, c).start())

        for k in range(1, GROUP_CHUNKS + 1):
            pl.when(n == k)(functools.partial(compute_span, 0, k, True))

        @pl.when(f == NF - 1)
        def _():
            for_chunks(n, lambda c: y_copy(start, c).start())

            @pl.when(g == G - 1)
            def _():
                for_chunks(n, lambda c: y_copy(start, c).wait())


def _expert_call(l, xs, ge, gs, gn, w_gate, w_up, w_down, b_gate, b_up, b_down, tf):
    rows = xs.shape[0]
    D, F = w_gate.shape[-2:]
    NF = F // tf
    G = ge.shape[0]

    def fsel(f, gn, g):
        return jnp.where(gn[g] > 0, f, NF - 1)

    return pl.pallas_call(
        functools.partial(_expert_kernel, NF=NF, G=G),
        grid_spec=pltpu.PrefetchScalarGridSpec(
            num_scalar_prefetch=3,
            grid=(G, NF),
            in_specs=[
                pl.BlockSpec(memory_space=pl.ANY),
                pl.BlockSpec((None, None, D, tf), lambda g, f, ge, gs, gn: (l, ge[g], 0, fsel(f, gn, g))),
                pl.BlockSpec((None, None, D, tf), lambda g, f, ge, gs, gn: (l, ge[g], 0, fsel(f, gn, g))),
                pl.BlockSpec((None, None, tf, D), lambda g, f, ge, gs, gn: (l, ge[g], fsel(f, gn, g), 0)),
                pl.BlockSpec((None, None, 1, tf), lambda g, f, ge, gs, gn: (l, ge[g], 0, fsel(f, gn, g))),
                pl.BlockSpec((None, None, 1, tf), lambda g, f, ge, gs, gn: (l, ge[g], 0, fsel(f, gn, g))),
                pl.BlockSpec((None, None, 1, D), lambda g, f, ge, gs, gn: (l, ge[g], 0, 0)),
            ],
            out_specs=pl.BlockSpec(memory_space=pl.ANY),
            scratch_shapes=[
                pltpu.VMEM((GROUP_CHUNKS, EXPERT_CHUNK, xs.shape[1]), xs.dtype),
                pltpu.VMEM((GROUP_CHUNKS * EXPERT_CHUNK, D), BF16),
                pltpu.VMEM((GROUP_CHUNKS * EXPERT_CHUNK, D), F32),
                pltpu.VMEM((D, tf), BF16),
                pltpu.VMEM((D, tf), BF16),
                pltpu.VMEM((tf, D), BF16),
                pltpu.SemaphoreType.DMA((GROUP_CHUNKS,)),
                pltpu.SemaphoreType.DMA((GROUP_CHUNKS,)),
            ],
        ),
        out_shape=jax.ShapeDtypeStruct((rows, D), F32),
        compiler_params=_cp(("arbitrary", "arbitrary")),
        name="grouped_experts",
    )(ge, gs, gn, xs, w_gate, w_up, w_down, b_gate, b_up, b_down)


def _combine_kernel(dest_ref, os_hbm, x1_ref, g2_ref, route_ref, o_ref, buf, sem, *, tm):
    i = pl.program_id(0)
    nt = pl.num_programs(0)

    def issue(tile, slot):
        def body(r, c):
            t = tile * tm + r
            for k in range(TOP_K):
                pltpu.make_async_copy(os_hbm.at[pl.ds(dest_ref[t * TOP_K + k], 1)],
                                      buf.at[slot, k, pl.ds(r, 1)], sem.at[slot]).start(priority=k % 2)
            return c

        lax.fori_loop(0, tm, body, 0, unroll=8)

    @pl.when(i == 0)
    def _():
        issue(0, 0)

    @pl.when(i + 1 < nt)
    def _():
        issue(i + 1, (i + 1) % 2)

    slot = i % 2
    for k in range(TOP_K):
        pltpu.make_async_copy(os_hbm.at[pl.ds(0, tm)], buf.at[slot, k], sem.at[slot]).wait()
    route = route_ref[...]
    y = route[:, 2 * TOP_K:2 * TOP_K + 1] * buf[slot, 0]
    for k in range(1, TOP_K):
        y = y + route[:, 2 * TOP_K + k:2 * TOP_K + k + 1] * buf[slot, k]
    o_ref[...] = x1_ref[...] + g2_ref[...] * y


def _combine_call(dest, os_, x1, modt, route, tm):
    T, D = x1.shape
    return pl.pallas_call(
        functools.partial(_combine_kernel, tm=tm),
        grid_spec=pltpu.PrefetchScalarGridSpec(
            num_scalar_prefetch=1,
            grid=(T // tm,),
            in_specs=[
                pl.BlockSpec(memory_space=pl.ANY),
                pl.BlockSpec((tm, D), lambda i, d: (i, 0)),
                pl.BlockSpec((None, 1, D), lambda i, d: (i, 0, 5)),
                pl.BlockSpec((tm, LANES), lambda i, d: (i, 0)),
            ],
            out_specs=pl.BlockSpec((tm, D), lambda i, d: (i, 0)),
            scratch_shapes=[pltpu.VMEM((2, TOP_K, tm, D), F32), pltpu.SemaphoreType.DMA((2,))],
        ),
        out_shape=jax.ShapeDtypeStruct((T, D), F32),
        compiler_params=_cp(("arbitrary",)),
        name="expert_combine",
    )(dest, os_, x1, modt, route)


def _rope_tables(pos_a, pos_b, half, reps):
    freqs = ROPE_BASE ** (-jnp.arange(half, dtype=F32) / half)

    def blk(pos):
        ang = pos.astype(F32)[:, None] * freqs[None, :]
        c, s = jnp.cos(ang), jnp.sin(ang)
        return jnp.concatenate([c, c], -1), jnp.concatenate([-s, s], -1)

    ca, sa = blk(pos_a)
    cb, sb = blk(pos_b)
    cos = jnp.tile(jnp.concatenate([ca, cb], -1), (1, reps))
    sin = jnp.tile(jnp.concatenate([sa, sb], -1), (1, reps))
    return cos, sin


def _cond_tiles(T, Tp, Ss, tm):
    starts = np.arange(0, T, tm)
    return np.where(starts < Tp, 0, 1 + np.maximum(starts - Tp, 0) // Ss).astype(np.int32)


def _pick_tile(limit, *sizes):
    t = ROW_ALIGN
    while t * 2 <= limit and all(s % (t * 2) == 0 for s in sizes):
        t *= 2
    return t


def _route_plan(route, cnt, E, G):
    idx = route[:, 0:TOP_K].astype(jnp.int32)
    rank = route[:, TOP_K:2 * TOP_K].astype(jnp.int32)
    counts = cnt[0, :E].astype(jnp.int32)
    nch = (counts + EXPERT_CHUNK - 1) // EXPERT_CHUNK
    padded = nch * EXPERT_CHUNK
    base = jnp.cumsum(padded) - padded
    dest = (base[idx] + rank).reshape(-1)
    ngr = (nch + GROUP_CHUNKS - 1) // GROUP_CHUNKS
    gend = jnp.cumsum(ngr)
    gid = jnp.arange(G, dtype=jnp.int32)
    valid = gid < gend[-1]
    ge = jnp.minimum(jnp.sum((gid[:, None] >= gend[None, :]).astype(jnp.int32), axis=1), E - 1)
    j = gid - (gend - ngr)[ge]
    gs = base[ge] + j * (GROUP_CHUNKS * EXPERT_CHUNK)
    gn = jnp.clip(nch[ge] - j * GROUP_CHUNKS, 0, GROUP_CHUNKS)
    last = jnp.maximum(gend[-1] - 1, 0)
    ge = jnp.where(valid, ge, ge[last])
    gs = jnp.where(valid, gs, 0)
    gn = jnp.where(valid, gn, 0)
    return (dest.astype(jnp.int32), (base + counts).astype(jnp.int32), (padded - counts).astype(jnp.int32),
            ge.astype(jnp.int32), gs.astype(jnp.int32), gn.astype(jnp.int32))


def kernel(x_prompt, x_sample, cache_diff_k, cache_diff_v, cache_mla_ckv, cache_mla_krope, state_ret_fwd, state_ret_bwd, c, c_ctx, w_mod, b_mod, norm1, norm2, w_in, diff_q_gain, diff_k_gain, diff_lambda, diff_subln, mla_q_gain, w_uq, mla_kv_gain, w_ukv, mla_qn_gain, mla_kn_gain, ret_decay, w_br_diff, w_br_mla, w_br_ret, w_o, w_router, b_router, w_gate, b_gate, w_up, b_up, w_down, b_down):
    Bp, Sp, D = x_prompt.shape
    Bs, Ss, _ = x_sample.shape
    L = w_mod.shape[0]
    P = cache_diff_k.shape[2]
    E = w_router.shape[-1]
    F = w_gate.shape[-1]
    Tp, Ts = Bp * Sp, Bs * Ss
    T = Tp + Ts
    assert 1 + Bs <= COND_ROWS and E <= LANES
    assert Tp % ROW_ALIGN == 0 and Ss % ROW_ALIGN == 0 and Tp % Ss == 0 and Sp % RET_CHUNK == 0
    assert (3 * D) % 512 == 0 and F % 512 == 0

    off = 3 * D
    o_gl = sum((512, 512, 512, 512, 256, 64, 256, 256, 512, 512, 512))
    w_in_b = w_in.astype(BF16)
    w_in_p = jnp.concatenate([
        w_in_b[..., o_gl:], w_in_b[..., 0:2304], w_in_b[..., 2368:2880], w_in_b[..., 2304:2368],
        jnp.zeros((L, D, 192), BF16), w_in_b[..., 2880:o_gl]], axis=-1)
    assert w_in_p.shape[-1] == off + _SEC_TOTAL
    cond = jnp.zeros((COND_ROWS, D), F32).at[0].set(c_ctx).at[1:1 + Bs].set(c)
    wuq_p = jnp.pad(w_uq, ((0, 0), (0, 0), (0, 0), (0, MLA_NOPE + MLA_VD - MLA_QK))).reshape(
        L, MLA_Q_RANK, -1).astype(BF16)
    wukv_p = w_ukv.reshape(L, MLA_KV_RANK, -1).astype(BF16)
    pad_qk = ((0, 0), (0, MLA_NOPE + MLA_VD - MLA_QK))
    mla_w = (wuq_p, wukv_p, mla_q_gain[:, None, :], mla_kv_gain[:, None, :],
             jnp.pad(mla_qn_gain, pad_qk)[:, None, :], jnp.pad(mla_kn_gain, pad_qk)[:, None, :])
    diff_g = (jnp.tile(diff_q_gain, (1, 2))[:, None, :], jnp.tile(diff_k_gain, (1, 2))[:, None, :],
              diff_subln[:, None, :])
    wd_b, wm_b, wr_b, wo_b = (w.astype(BF16) for w in (w_br_diff, w_br_mla, w_br_ret, w_o))
    w_router_p = jnp.pad(w_router, ((0, 0), (0, 0), (0, LANES - E)))
    b_router_p = jnp.pad(b_router, ((0, 0), (0, LANES - E)))[:, None, :]
    lam_vec = diff_lambda.astype(F32)
    lam_init = jnp.asarray([0.8 - 0.6 * math.exp(-0.3 * l) for l in range(L)], F32)
    lam = (jnp.exp(jnp.sum(lam_vec[:, 0] * lam_vec[:, 1], -1))
           - jnp.exp(jnp.sum(lam_vec[:, 2] * lam_vec[:, 3], -1)) + lam_init)
    log_g = jax.nn.log_sigmoid(ret_decay.astype(F32)).reshape(-1)
    cdk = cache_diff_k.reshape(Bs, L, P, DIFF_HEADS * 2 * DIFF_HD)
    cdv = cache_diff_v.reshape(Bs, L, P, DIFF_HEADS * DIFF_VD)
    ckr = jnp.pad(cache_mla_krope, ((0, 0), (0, 0), (0, 0), (0, LANES - MLA_ROPE)))
    t_pos = jnp.arange(Ss, dtype=jnp.int32)
    tabs_ax = _rope_tables(t_pos // GRID_W, t_pos % GRID_W, 16, 2)
    tabs_1d = _rope_tables(t_pos, t_pos, 32, 1)
    bg4, bu4, bd4 = b_gate[:, :, None, :], b_up[:, :, None, :], b_down[:, :, None, :]

    tm_p = _pick_tile(1024, Tp, Ss)
    tn_p = 1536 if (off + _SEC_TOTAL) % 1536 == 0 else 512
    tm_r = ROW_ALIGN
    tq = ROW_ALIGN
    rows_sorted = T * TOP_K + E * EXPERT_CHUNK
    G = E + (T * TOP_K) // (GROUP_CHUNKS * EXPERT_CHUNK)
    tf = 512

    mod = _mod_call(cond, w_mod, b_mod)
    x = jnp.concatenate([x_prompt.reshape(Tp, D), x_sample.reshape(Ts, D)], axis=0)
    outs = [[] for _ in range(6)]
    for l in range(L):
        modt_p = mod[l][_cond_tiles(T, Tp, Ss, tm_p)][:, None, :]
        modt_r = mod[l][_cond_tiles(T, Tp, Ss, tm_r)][:, None, :]
        proj = _proj_call(l, x, modt_p, norm1[:, None, :], w_in_p, tm_p, tn_p)

        yd_p, kn_p = _diff_call(l, proj, off, lam, diff_g, Bp, Sp, 0, min(tq, Sp), False)
        (yd_s,) = _diff_call(l, proj, off, lam, diff_g, Bs, Ss, Tp, tq, True, (cdk, cdv), tabs_ax)
        ym_p, ckv_p = _mla_call(l, proj, off, mla_w, Bp, Sp, 0, min(tq, Sp), False)
        (ym_s,) = _mla_call(l, proj, off, mla_w, Bs, Ss, Tp, tq, True, (cache_mla_ckv, ckr), tabs_ax)
        yr_p, sf_p, sb_p = _ret_call(l, proj, off, log_g, Bp, Sp, 0, False)
        (yr_s,) = _ret_call(l, proj, off, log_g, Bs, Ss, Tp, True, (state_ret_fwd, state_ret_bwd), tabs_1d)

        yd = jnp.concatenate([yd_p, yd_s], axis=0)
        ym = jnp.concatenate([ym_p, ym_s], axis=0)
        yr = jnp.concatenate([yr_p, yr_s], axis=0)
        merged = _merge_call(l, yd, ym, yr, proj, wd_b, wm_b, wr_b, tm_r)
        x1, h2, route, cnt = _post_call(l, merged, x, wo_b, modt_r, norm2[:, None, :],
                                        w_router_p, b_router_p, E, tm_r)
        dest, pstart, pcnt, ge, gs, gn = _route_plan(route, cnt, E, G)
        xs = _dispatch_call(h2, dest, pstart, pcnt, rows_sorted, E, tm_r)
        os_ = _expert_call(l, xs, ge, gs, gn, w_gate, w_up, w_down, bg4, bu4, bd4, tf)
        x = _combine_call(dest, os_, x1, modt_r, route, tm_r)

        dv = proj[:Tp, off + _SEC["DV"]:off + _SEC["DV"] + DIFF_HEADS * DIFF_VD]
        kr = proj[:Tp, off + _SEC["MKR"]:off + _SEC["MKR"] + MLA_ROPE]
        for lst, v in zip(outs, (kn_p.reshape(Bp, Sp, DIFF_HEADS, 2 * DIFF_HD),
                                 dv.reshape(Bp, Sp, DIFF_HEADS, DIFF_VD),
                                 ckv_p.reshape(Bp, Sp, MLA_KV_RANK), kr.reshape(Bp, Sp, MLA_ROPE),
                                 sf_p, sb_p)):
            lst.append(v)

    y_prompt = x[:Tp].reshape(Bp, Sp, D)
    y_sample = x[Tp:].reshape(Bs, Ss, D)
    return (y_prompt, y_sample) + tuple(jnp.stack(o, axis=1) for o in outs)
```

```python
import functools
import math

import jax
import jax.numpy as jnp
import numpy as np
from jax import lax
from jax.experimental import pallas as pl
from jax.experimental.pallas import tpu as pltpu

F32 = jnp.float32
BF16 = jnp.bfloat16

GRID_W = 64
DIFF_HEADS = 4
DIFF_HD = 64
DIFF_VD = 128
MLA_HEADS = 8
MLA_Q_RANK = 512
MLA_KV_RANK = 256
MLA_NOPE = 128
MLA_ROPE = 64
MLA_VD = 128
MLA_QK = MLA_NOPE + MLA_ROPE
RET_HEADS = 4
RET_QK = 64
RET_VD = 128
RET_CHUNK = 128
TOP_K = 4
SWIGLU_LIMIT = 7.0
SWIGLU_ALPHA = 1.702
ROPE_BASE = 10000.0
EPS = 1e-6

LANES = 128
ROW_ALIGN = 256
EXPERT_CHUNK = 256
GROUP_CHUNKS = 4
COND_ROWS = 8
VMEM_LIMIT = 56 * 1024 * 1024

_SEC = dict(DQ=0, DK=512, DV=1024, MQ=1536, MKV=2048, RQ=2304, RK=2560, MKR=2816,
            RV=3072, RGF=3584, RGB=4096)
_SEC_TOTAL = 4608


def _cp(sem, vmem=VMEM_LIMIT):
    return pltpu.CompilerParams(dimension_semantics=sem, vmem_limit_bytes=vmem)


def _lane_iota(shape):
    return lax.broadcasted_iota(jnp.int32, shape, len(shape) - 1)


def _rms(x, n=None):
    n = x.shape[-1] if n is None else n
    return x * lax.rsqrt(jnp.sum(x * x, axis=-1, keepdims=True) / n + EPS)


def _dot(a, b):
    return jnp.dot(a, b, preferred_element_type=F32)


def _dot_nt(a, b):
    return lax.dot_general(a, b, (((1,), (1,)), ((), ())), preferred_element_type=F32)


def _softmax(s):
    e = jnp.exp(s - jnp.max(s, axis=-1, keepdims=True))
    return e / jnp.sum(e, axis=-1, keepdims=True)


def _sigmoid(x):
    return 1.0 / (1.0 + jnp.exp(-x))


def _rope(x, cos, sin, half):
    lane = _lane_iota(x.shape)
    up = pltpu.roll(x, LANES - half, 1)
    dn = pltpu.roll(x, half, 1)
    sw = jnp.where(lane % (2 * half) < half, up, dn)
    return x * cos + sw * sin


def _norm_halves(x, gain):
    lane = _lane_iota(x.shape)
    lo = lane < DIFF_HD
    x2 = x * x
    ss_lo = jnp.sum(jnp.where(lo, x2, 0.0), axis=-1, keepdims=True)
    ss_hi = jnp.sum(jnp.where(lo, 0.0, x2), axis=-1, keepdims=True)
    inv = jnp.where(lo, lax.rsqrt(ss_lo / DIFF_HD + EPS), lax.rsqrt(ss_hi / DIFF_HD + EPS))
    return x * inv * gain


def _mod_kernel(c_ref, w_ref, b_ref, o_ref):
    c = c_ref[...]
    a = (c * _sigmoid(c)).astype(BF16)
    o_ref[...] = _dot(a, w_ref[...].astype(BF16)) + b_ref[...]


def _mod_call(cond, w_mod, b_mod):
    L, D, N = w_mod.shape
    tn = 1024 if N % 1024 == 0 else 512
    return pl.pallas_call(
        _mod_kernel,
        grid=(L, N // tn),
        in_specs=[
            pl.BlockSpec((COND_ROWS, D), lambda l, j: (0, 0)),
            pl.BlockSpec((None, D, tn), lambda l, j: (l, 0, j)),
            pl.BlockSpec((None, 1, tn), lambda l, j: (l, 0, j)),
        ],
        out_specs=pl.BlockSpec((None, COND_ROWS, tn), lambda l, j: (l, 0, j)),
        out_shape=jax.ShapeDtypeStruct((L, COND_ROWS, N), F32),
        compiler_params=_cp(("arbitrary", "arbitrary")),
        name="adaln_mod",
    )(cond, w_mod, b_mod.reshape(L, 1, N))


def _proj_kernel(x_ref, sc_ref, sh_ref, g_ref, w_ref, o_ref, h_ref):
    @pl.when(pl.program_id(1) == 0)
    def _():
        h = _rms(x_ref[...]) * g_ref[...]
        h_ref[...] = (h * (1.0 + sc_ref[...]) + sh_ref[...]).astype(BF16)

    o_ref[...] = _dot(h_ref[...], w_ref[...])


def _proj_call(l, x, modt, norm1, w_in_p, tm, tn):
    T, D = x.shape
    PC = w_in_p.shape[-1]
    return pl.pallas_call(
        _proj_kernel,
        grid=(T // tm, PC // tn),
        in_specs=[
            pl.BlockSpec((tm, D), lambda i, j: (i, 0)),
            pl.BlockSpec((None, 1, D), lambda i, j: (i, 0, 1)),
            pl.BlockSpec((None, 1, D), lambda i, j: (i, 0, 0)),
            pl.BlockSpec((None, 1, D), lambda i, j: (l, 0, 0)),
            pl.BlockSpec((None, D, tn), lambda i, j: (l, 0, j)),
        ],
        out_specs=pl.BlockSpec((tm, tn), lambda i, j: (i, j)),
        out_shape=jax.ShapeDtypeStruct((T, PC), F32),
        scratch_shapes=[pltpu.VMEM((tm, D), BF16)],
        compiler_params=_cp(("arbitrary", "arbitrary")),
        name="norm_proj",
    )(x, modt, modt, norm1, w_in_p)


def _diff_kernel(*refs, latent, S, P, lam_init, l):
    if latent:
        (lam_ref, q_ref, k_ref, v_ref, qg_ref, kg_ref, sg_ref, ck_ref, cv_ref,
         cosq_ref, sinq_ref, cosk_ref, sink_ref, y_ref, kall, vall) = refs
    else:
        (lam_ref, q_ref, k_ref, v_ref, qg_ref, kg_ref, sg_ref, y_ref, kn_ref, kall, vall) = refs

    @pl.when(pl.program_id(1) == 0)
    def _():
        for h in range(DIFF_HEADS):
            sl = slice(LANES * h, LANES * (h + 1))
            kn = _norm_halves(k_ref[:, sl], kg_ref[...])
            if latent:
                kn = _rope(kn, cosk_ref[...], sink_ref[...], 16)
            else:
                kn_ref[:, sl] = kn
            kall[0:S, sl] = kn.astype(BF16)
        vall[0:S, :] = v_ref[...].astype(BF16)
        if latent:
            kall[S:S + P, :] = ck_ref[...].astype(BF16)
            vall[S:S + P, :] = cv_ref[...].astype(BF16)

    lam = lam_ref[l]
    scale = DIFF_HD ** -0.5
    for h in range(DIFF_HEADS):
        sl = slice(LANES * h, LANES * (h + 1))
        qn = _norm_halves(q_ref[:, sl], qg_ref[...])
        if latent:
            qn = _rope(qn, cosq_ref[...], sinq_ref[...], 16)
        lo = _lane_iota(qn.shape) < DIFF_HD
        q1 = jnp.where(lo, qn, 0.0).astype(BF16)
        q2 = jnp.where(lo, 0.0, qn).astype(BF16)
        kh = kall[:, sl]
        p = _softmax(_dot_nt(q1, kh) * scale) - lam * _softmax(_dot_nt(q2, kh) * scale)
        o = _dot(p.astype(BF16), vall[:, sl])
        y_ref[:, sl] = (_rms(o) * sg_ref[...] * (1.0 - lam_init)).astype(BF16)


def _diff_call(l, proj, off, lam, gains, B, S, row0, tq, latent, cache=None, tabs=None):
    qg, kg, sg = gains
    W = DIFF_HEADS * LANES
    nq = S // tq
    P = cache[0].shape[2] if latent else 0
    lam_init = 0.8 - 0.6 * math.exp(-0.3 * l)
    cq, ck, cv = ((off + _SEC[n]) // W for n in ("DQ", "DK", "DV"))
    in_specs = [
        pl.BlockSpec(memory_space=pltpu.SMEM),
        pl.BlockSpec((tq, W), lambda b, i: (row0 // tq + b * nq + i, cq)),
        pl.BlockSpec((S, W), lambda b, i: (row0 // S + b, ck)),
        pl.BlockSpec((S, W), lambda b, i: (row0 // S + b, cv)),
        pl.BlockSpec((None, 1, LANES), lambda b, i: (l, 0, 0)),
        pl.BlockSpec((None, 1, LANES), lambda b, i: (l, 0, 0)),
        pl.BlockSpec((None, 1, LANES), lambda b, i: (l, 0, 0)),
    ]
    args = [lam, proj, proj, proj, qg, kg, sg]
    out_shape = [jax.ShapeDtypeStruct((B * S, W), BF16)]
    out_specs = [pl.BlockSpec((tq, W), lambda b, i: (b * nq + i, 0))]
    if latent:
        cos, sin = tabs
        in_specs += [
            pl.BlockSpec((None, None, P, W), lambda b, i: (b, l, 0, 0)),
            pl.BlockSpec((None, None, P, W), lambda b, i: (b, l, 0, 0)),
            pl.BlockSpec((tq, LANES), lambda b, i: (i, 0)),
            pl.BlockSpec((tq, LANES), lambda b, i: (i, 0)),
            pl.BlockSpec((S, LANES), lambda b, i: (0, 0)),
            pl.BlockSpec((S, LANES), lambda b, i: (0, 0)),
        ]
        args += [cache[0], cache[1], cos, sin, cos, sin]
    else:
        out_shape.append(jax.ShapeDtypeStruct((B * S, W), F32))
        out_specs.append(pl.BlockSpec((S, W), lambda b, i: (b, 0)))
    return pl.pallas_call(
        functools.partial(_diff_kernel, latent=latent, S=S, P=P, lam_init=lam_init, l=l),
        grid=(B, nq),
        in_specs=in_specs,
        out_specs=out_specs,
        out_shape=out_shape,
        scratch_shapes=[pltpu.VMEM((S + P, W), BF16), pltpu.VMEM((S + P, W), BF16)],
        compiler_params=_cp(("arbitrary", "arbitrary")),
        name="diff_attn_lat" if latent else "diff_attn_ctx",
    )(*args)


def _mla_kernel(*refs, latent, S, P):
    if latent:
        (mq_ref, mkv_ref, mkr_ref, wuq_ref, wukv_ref, qg_ref, kvg_ref, qng_ref, kng_ref,
         cckv_ref, ckr_ref, cosq_ref, sinq_ref, cosk_ref, sink_ref, y_ref, kall, vall) = refs
    else:
        (mq_ref, mkv_ref, mkr_ref, wuq_ref, wukv_ref, qg_ref, kvg_ref, qng_ref, kng_ref,
         y_ref, ckv_ref, kall, vall) = refs
    HW = MLA_NOPE + MLA_VD

    @pl.when(pl.program_id(1) == 0)
    def _():
        ckv = _rms(mkv_ref[...]) * kvg_ref[...]
        if not latent:
            ckv_ref[...] = ckv
        segs = [(0, S, ckv, mkr_ref[...], latent)]
        if latent:
            segs.append((S, P, cckv_ref[...], ckr_ref[...], False))
        for r0, R, cv, kr, rot in segs:
            kv = _dot(cv.astype(BF16), wukv_ref[...])
            kr_ss = jnp.sum(kr * kr, axis=-1, keepdims=True)
            krg = kr * kng_ref[:, MLA_NOPE:]
            if rot:
                krg = _rope(krg, cosk_ref[...], sink_ref[...], 16)
            for h in range(MLA_HEADS):
                kn = kv[:, HW * h:HW * h + MLA_NOPE]
                ss = jnp.sum(kn * kn, axis=-1, keepdims=True) + kr_ss
                r = lax.rsqrt(ss / MLA_QK + EPS)
                kall[h, r0:r0 + R, 0:MLA_NOPE] = (kn * r * kng_ref[:, 0:MLA_NOPE]).astype(BF16)
                kall[h, r0:r0 + R, MLA_NOPE:] = (krg * r).astype(BF16)
                vall[h, r0:r0 + R, :] = kv[:, HW * h + MLA_NOPE:HW * (h + 1)].astype(BF16)

    cq = (_rms(mq_ref[...]) * qg_ref[...]).astype(BF16)
    qm = _dot(cq, wuq_ref[...])
    scale = MLA_QK ** -0.5
    for h in range(MLA_HEADS):
        qn = _rms(qm[:, HW * h:HW * (h + 1)], MLA_QK) * qng_ref[...]
        if latent:
            qr = _rope(qn[:, MLA_NOPE:], cosq_ref[...], sinq_ref[...], 16)
            qn = jnp.concatenate([qn[:, 0:MLA_NOPE], qr], axis=-1)
        p = _softmax(_dot_nt(qn.astype(BF16), kall[h]) * scale)
        y_ref[:, MLA_VD * h:MLA_VD * (h + 1)] = _dot(p.astype(BF16), vall[h]).astype(BF16)


def _mla_call(l, proj, off, weights, B, S, row0, tq, latent, cache=None, tabs=None):
    wuq, wukv, qg, kvg, qng, kng = weights
    nq = S // tq
    P = cache[0].shape[2] if latent else 0
    HW = MLA_NOPE + MLA_VD
    NW = MLA_HEADS * HW
    c_mq = (off + _SEC["MQ"]) // MLA_Q_RANK
    c_mkv = (off + _SEC["MKV"]) // MLA_KV_RANK
    c_mkr = (off + _SEC["MKR"]) // LANES

    def lay(shape):
        return pl.BlockSpec((None,) + shape, lambda b, i: (l,) + (0,) * len(shape))

    in_specs = [
        pl.BlockSpec((tq, MLA_Q_RANK), lambda b, i: (row0 // tq + b * nq + i, c_mq)),
        pl.BlockSpec((S, MLA_KV_RANK), lambda b, i: (row0 // S + b, c_mkv)),
        pl.BlockSpec((S, LANES), lambda b, i: (row0 // S + b, c_mkr)),
        lay((MLA_Q_RANK, NW)), lay((MLA_KV_RANK, NW)),
        lay((1, MLA_Q_RANK)), lay((1, MLA_KV_RANK)), lay((1, HW)), lay((1, HW)),
    ]
    args = [proj, proj, proj, wuq, wukv, qg, kvg, qng, kng]
    out_shape = [jax.ShapeDtypeStruct((B * S, MLA_HEADS * MLA_VD), BF16)]
    out_specs = [pl.BlockSpec((tq, MLA_HEADS * MLA_VD), lambda b, i: (b * nq + i, 0))]
    if latent:
        cos, sin = tabs
        in_specs += [
            pl.BlockSpec((None, None, P, MLA_KV_RANK), lambda b, i: (b, l, 0, 0)),
            pl.BlockSpec((None, None, P, LANES), lambda b, i: (b, l, 0, 0)),
            pl.BlockSpec((tq, LANES), lambda b, i: (i, 0)),
            pl.BlockSpec((tq, LANES), lambda b, i: (i, 0)),
            pl.BlockSpec((S, LANES), lambda b, i: (0, 0)),
            pl.BlockSpec((S, LANES), lambda b, i: (0, 0)),
        ]
        args += [cache[0], cache[1], cos, sin, cos, sin]
    else:
        out_shape.append(jax.ShapeDtypeStruct((B * S, MLA_KV_RANK), F32))
        out_specs.append(pl.BlockSpec((S, MLA_KV_RANK), lambda b, i: (b, 0)))
    return pl.pallas_call(
        functools.partial(_mla_kernel, latent=latent, S=S, P=P),
        grid=(B, nq),
        in_specs=in_specs,
        out_specs=out_specs,
        out_shape=out_shape,
        scratch_shapes=[pltpu.VMEM((MLA_HEADS, S + P, HW), BF16),
                        pltpu.VMEM((MLA_HEADS, S + P, MLA_VD), BF16)],
        compiler_params=_cp(("arbitrary", "arbitrary")),
        name="mla_lat" if latent else "mla_ctx",
    )(*args)


def _ret_kernel(*refs, latent, S, l):
    if latent:
        (lg_ref, rq_ref, rk_ref, rv_ref, gf_ref, gb_ref, cos_ref, sin_ref, s0f_ref, s0b_ref,
         y_ref, qs, ks, of, ob, st, dm, dq, dk) = refs
    else:
        (lg_ref, rq_ref, rk_ref, rv_ref, gf_ref, gb_ref, y_ref, sf_ref, sb_ref,
         qs, ks, of, ob, st, dm, dq, dk) = refs
    C = RET_CHUNK
    H = RET_HEADS
    nc = S // C

    @pl.when(pl.program_id(0) == 0)
    def _():
        ii = lax.broadcasted_iota(jnp.int32, (C, C), 0).astype(F32)
        jj = lax.broadcasted_iota(jnp.int32, (C, C), 1).astype(F32)
        row = lax.broadcasted_iota(jnp.int32, (C, LANES), 0).astype(F32)
        for d in range(2):
            for h in range(H):
                lg = lg_ref[(l * 2 + d) * H + h]
                rel = ii - jj if d == 0 else jj - ii
                dm[d * H + h] = jnp.where(rel >= 0, jnp.exp(jnp.maximum(rel, 0.0) * lg), 0.0)
                dq[d * H + h] = jnp.exp(((row + 1.0) if d == 0 else (C - row)) * lg)
                dk[d * H + h] = jnp.exp(((C - 1.0 - row) if d == 0 else row) * lg)

    for s in range(H // 2):
        ssl = slice(LANES * s, LANES * (s + 1))
        qslab = rq_ref[:, ssl]
        kslab = rk_ref[:, ssl] * (RET_QK ** -0.5)
        if latent:
            qslab = _rope(qslab, cos_ref[...], sin_ref[...], 32)
            kslab = _rope(kslab, cos_ref[...], sin_ref[...], 32)
        qs[:, ssl] = qslab
        ks[:, ssl] = kslab
    for d in range(2):
        for h in range(H):
            if latent:
                s0 = (s0f_ref if d == 0 else s0b_ref)[h]
                z = jnp.zeros_like(s0)
                st[d * H + h] = jnp.concatenate([s0, z] if h % 2 == 0 else [z, s0], axis=0)
            else:
                st[d * H + h] = jnp.zeros((LANES, RET_VD), F32)

    lo = _lane_iota((C, LANES)) < RET_QK
    one = jnp.ones((1, 1), F32)

    def body(n, carry):
        for d in range(2):
            r0 = pl.multiple_of((n if d == 0 else nc - 1 - n) * C, C)
            for h in range(H):
                i = d * H + h
                ssl = slice(LANES * (h // 2), LANES * (h // 2 + 1))
                vsl = slice(RET_VD * h, RET_VD * (h + 1))
                keep = lo if h % 2 == 0 else jnp.logical_not(lo)
                qc = jnp.where(keep, qs[pl.ds(r0, C), ssl], 0.0)
                kc = jnp.where(keep, ks[pl.ds(r0, C), ssl], 0.0)
                vc = rv_ref[pl.ds(r0, C), vsl].astype(BF16)
                sc = _dot_nt(qc.astype(BF16), kc.astype(BF16)) * dm[i]
                stv = st[i]
                o = _dot(sc.astype(BF16), vc) + _dot((qc * dq[i]).astype(BF16), stv.astype(BF16))
                (of if d == 0 else ob)[pl.ds(r0, C), vsl] = o
                kt = jnp.transpose(kc * dk[i]).astype(BF16)
                cdec = jnp.exp(one * (C * lg_ref[(l * 2 + d) * H + h]))
                st[i] = stv * cdec + _dot(kt, vc)
        return carry

    lax.fori_loop(0, nc, body, 0)

    for h in range(H):
        vsl = slice(RET_VD * h, RET_VD * (h + 1))
        if not latent:
            sf_ref[h] = st[h, RET_QK * (h % 2):RET_QK * (h % 2 + 1), :]
            sb_ref[h] = st[H + h, RET_QK * (h % 2):RET_QK * (h % 2 + 1), :]
        gf = gf_ref[:, vsl]
        gb = gb_ref[:, vsl]
        y = _rms(of[:, vsl]) * (gf * _sigmoid(gf)) + _rms(ob[:, vsl]) * (gb * _sigmoid(gb))
        y_ref[:, vsl] = y.astype(BF16)


def _ret_call(l, proj, off, lg, B, S, row0, latent, states=None, tabs=None):
    QW = RET_HEADS * RET_QK
    VW = RET_HEADS * RET_VD
    c_rq, c_rk = (off + _SEC["RQ"]) // QW, (off + _SEC["RK"]) // QW
    c_rv, c_gf, c_gb = ((off + _SEC[n]) // VW for n in ("RV", "RGF", "RGB"))
    rb = row0 // S
    in_specs = [
        pl.BlockSpec(memory_space=pltpu.SMEM),
        pl.BlockSpec((S, QW), lambda b: (rb + b, c_rq)),
        pl.BlockSpec((S, QW), lambda b: (rb + b, c_rk)),
        pl.BlockSpec((S, VW), lambda b: (rb + b, c_rv)),
        pl.BlockSpec((S, VW), lambda b: (rb + b, c_gf)),
        pl.BlockSpec((S, VW), lambda b: (rb + b, c_gb)),
    ]
    args = [lg, proj, proj, proj, proj, proj]
    out_shape = [jax.ShapeDtypeStruct((B * S, VW), BF16)]
    out_specs = [pl.BlockSpec((S, VW), lambda b: (b, 0))]
    st_block = (None, RET_HEADS, RET_QK, RET_VD)
    if latent:
        cos, sin = tabs
        in_specs += [
            pl.BlockSpec((S, LANES), lambda b: (0, 0)),
            pl.BlockSpec((S, LANES), lambda b: (0, 0)),
            pl.BlockSpec((None,) + st_block, lambda b: (b, l, 0, 0, 0)),
            pl.BlockSpec((None,) + st_block, lambda b: (b, l, 0, 0, 0)),
        ]
        args += [cos, sin, states[0], states[1]]
    else:
        for _ in range(2):
            out_shape.append(jax.ShapeDtypeStruct((B, RET_HEADS, RET_QK, RET_VD), F32))
            out_specs.append(pl.BlockSpec(st_block, lambda b: (b, 0, 0, 0)))
    return pl.pallas_call(
        functools.partial(_ret_kernel, latent=latent, S=S, l=l),
        grid=(B,),
        in_specs=in_specs,
        out_specs=out_specs,
        out_shape=out_shape,
        scratch_shapes=[pltpu.VMEM((S, QW), F32), pltpu.VMEM((S, QW), F32),
                        pltpu.VMEM((S, VW), F32), pltpu.VMEM((S, VW), F32),
                        pltpu.VMEM((2 * RET_HEADS, LANES, RET_VD), F32),
                        pltpu.VMEM((2 * RET_HEADS, RET_CHUNK, RET_CHUNK), F32),
                        pltpu.VMEM((2 * RET_HEADS, RET_CHUNK, LANES), F32),
                        pltpu.VMEM((2 * RET_HEADS, RET_CHUNK, LANES), F32)],
        compiler_params=_cp(("arbitrary",)),
        name="retention_lat" if latent else "retention_ctx",
    )(*args)


def _merge_kernel(yd_ref, ym_ref, yr_ref, g0_ref, g1_ref, g2_ref, wd_ref, wm_ref, wr_ref, o_ref):
    acc = _sigmoid(g0_ref[...]) * _dot(yd_ref[...], wd_ref[...])
    acc = acc + _sigmoid(g1_ref[...]) * _dot(ym_ref[...], wm_ref[...])
    acc = acc + _sigmoid(g2_ref[...]) * _dot(yr_ref[...], wr_ref[...])
    o_ref[...] = acc.astype(BF16)


def _merge_call(l, yd, ym, yr, proj, wd, wm, wr, tm):
    T = yd.shape[0]
    D = wd.shape[-1]

    def rows(w):
        return pl.BlockSpec((tm, w), lambda i: (i, 0))

    def gate(k):
        return pl.BlockSpec((tm, D), lambda i: (i, k))

    def wt(w):
        return pl.BlockSpec((None, w.shape[1], D), lambda i: (l, 0, 0))

    return pl.pallas_call(
        _merge_kernel,
        grid=(T // tm,),
        in_specs=[rows(yd.shape[1]), rows(ym.shape[1]), rows(yr.shape[1]),
                  gate(0), gate(1), gate(2), wt(wd), wt(wm), wt(wr)],
        out_specs=pl.BlockSpec((tm, D), lambda i: (i, 0)),
        out_shape=jax.ShapeDtypeStruct((T, D), BF16),
        compiler_params=_cp(("arbitrary",)),
        name="branch_merge",
    )(yd, ym, yr, proj, proj, proj, wd, wm, wr)


def _post_kernel(m_ref, x_ref, wo_ref, g1_ref, sc_ref, sh_ref, n2_ref, wr_ref, br_ref,
                 x1_ref, h2_ref, route_ref, cnt_ref, run_ref, *, E, tm):
    @pl.when(pl.program_id(0) == 0)
    def _():
        run_ref[...] = jnp.zeros_like(run_ref)

    x1 = x_ref[...] + g1_ref[...] * _dot(m_ref[...], wo_ref[...])
    x1_ref[...] = x1
    h = _rms(x1) * n2_ref[...]
    h = h * (1.0 + sc_ref[...]) + sh_ref[...]
    h2_ref[...] = h
    w = wr_ref[...]
    h_hi, w_hi = h.astype(BF16), w.astype(BF16)
    h_lo = (h - h_hi.astype(F32)).astype(BF16)
    w_lo = (w - w_hi.astype(F32)).astype(BF16)
    logits = _dot(h_hi, w_hi) + (_dot(h_hi, w_lo) + _dot(h_lo, w_hi)) + br_ref[...]
    lane = _lane_iota(logits.shape)
    cur = jnp.where(lane < E, logits, -jnp.inf)
    hots, vals = [], []
    for _ in range(TOP_K):
        m = jnp.max(cur, axis=-1, keepdims=True)
        idx = jnp.min(jnp.where(cur == m, lane, LANES), axis=-1, keepdims=True)
        hot = lane == idx
        hots.append(hot)
        vals.append(m)
        cur = jnp.where(hot, -jnp.inf, cur)
    exps = [jnp.exp(v - vals[0]) for v in vals]
    den = exps[0] + exps[1] + exps[2] + exps[3]
    sel = jnp.zeros(logits.shape, F32)
    for hot in hots:
        sel = sel + jnp.where(hot, 1.0, 0.0)
    ri = lax.broadcasted_iota(jnp.int32, (tm, tm), 0)
    ci = lax.broadcasted_iota(jnp.int32, (tm, tm), 1)
    tri = jnp.where(ci < ri, 1.0, 0.0).astype(BF16)
    rank = _dot(tri, sel.astype(BF16)) + run_ref[0:1, :]
    lane_f = lane.astype(F32)
    route = jnp.zeros(logits.shape, F32)
    for k in range(TOP_K):
        e_k = jnp.sum(jnp.where(hots[k], lane_f, 0.0), axis=-1, keepdims=True)
        r_k = jnp.sum(jnp.where(hots[k], rank, 0.0), axis=-1, keepdims=True)
        route = jnp.where(lane == k, e_k, route)
        route = jnp.where(lane == TOP_K + k, r_k, route)
        route = jnp.where(lane == 2 * TOP_K + k, exps[k] / den, route)
    route_ref[...] = route
    total = run_ref[...] + jnp.sum(sel, axis=0, keepdims=True)
    run_ref[...] = total
    cnt_ref[...] = total


def _post_call(l, merged, x, wo, modt, norm2, w_router, b_router, E, tm):
    T, D = x.shape

    def modc(k):
        return pl.BlockSpec((None, 1, D), lambda i: (i, 0, k))

    return pl.pallas_call(
        functools.partial(_post_kernel, E=E, tm=tm),
        grid=(T // tm,),
        in_specs=[
            pl.BlockSpec((tm, D), lambda i: (i, 0)),
            pl.BlockSpec((tm, D), lambda i: (i, 0)),
            pl.BlockSpec((None, D, D), lambda i: (l, 0, 0)),
            modc(2), modc(4), modc(3),
            pl.BlockSpec((None, 1, D), lambda i: (l, 0, 0)),
            pl.BlockSpec((None, D, LANES), lambda i: (l, 0, 0)),
            pl.BlockSpec((None, 1, LANES), lambda i: (l, 0, 0)),
        ],
        out_specs=[
            pl.BlockSpec((tm, D), lambda i: (i, 0)),
            pl.BlockSpec((tm, D), lambda i: (i, 0)),
            pl.BlockSpec((tm, LANES), lambda i: (i, 0)),
            pl.BlockSpec((COND_ROWS, LANES), lambda i: (0, 0)),
        ],
        out_shape=[
            jax.ShapeDtypeStruct((T, D), F32),
            jax.ShapeDtypeStruct((T, D), F32),
            jax.ShapeDtypeStruct((T, LANES), F32),
            jax.ShapeDtypeStruct((COND_ROWS, LANES), F32),
        ],
        scratch_shapes=[pltpu.VMEM((COND_ROWS, LANES), F32)],
        compiler_params=_cp(("arbitrary",)),
        name="out_proj_router",
    )(merged, x, wo, modt, modt, modt, norm2, w_router, b_router)


def _dispatch_kernel(dest_ref, pstart_ref, pcnt_ref, h_ref, xs_hbm, sem, *, tm, E):
    i = pl.program_id(0)

    def row_copy(r, dst):
        return pltpu.make_async_copy(h_ref.at[pl.ds(r, 1)], xs_hbm.at[pl.ds(dst, 1)], sem)

    def body(r, c):
        t = i * tm + r
        for k in range(TOP_K):
            row_copy(r, dest_ref[t * TOP_K + k]).start(priority=k % 2)
        return c

    lax.fori_loop(0, tm, body, 0, unroll=8)

    @pl.when(i == 0)
    def _():
        def pad_body(e, c):
            s = pstart_ref[e]
            n = pcnt_ref[e]
            lax.fori_loop(0, n, lambda r, c2: (row_copy(0, s + r).start(), c2)[1], 0)
            lax.fori_loop(0, n, lambda r, c2: (row_copy(0, 0).wait(), c2)[1], 0)
            return c

        lax.fori_loop(0, E, pad_body, 0)

    for _ in range(TOP_K):
        pltpu.make_async_copy(h_ref, xs_hbm.at[pl.ds(0, tm)], sem).wait()


def _dispatch_call(h2, dest, pstart, pcnt, rows, E, tm):
    T, D = h2.shape
    return pl.pallas_call(
        functools.partial(_dispatch_kernel, tm=tm, E=E),
        grid_spec=pltpu.PrefetchScalarGridSpec(
            num_scalar_prefetch=3,
            grid=(T // tm,),
            in_specs=[pl.BlockSpec((tm, D), lambda i, d, ps, pc: (i, 0))],
            out_specs=pl.BlockSpec(memory_space=pl.ANY),
            scratch_shapes=[pltpu.SemaphoreType.DMA],
        ),
        out_shape=jax.ShapeDtypeStruct((rows, D), h2.dtype),
        compiler_params=_cp(("arbitrary",)),
        name="expert_dispatch",
    )(dest, pstart, pcnt, h2)


def _expert_kernel(ge_ref, gs_ref, gn_ref, xs_hbm, wg_ref, wu_ref, wd_ref, bg_ref, bu_ref, bd_ref,
                   os_hbm, xland, xb, ybuf, wgb, wub, wdb, sem_in, sem_out, *, NF, G):
    g = pl.program_id(0)
    f = pl.program_id(1)
    n = gn_ref[g]
    start = gs_ref[g]
    g_next = jnp.minimum(g + 1, G - 1)
    n_next = jnp.where(g + 1 < G, gn_ref[g_next], 0)
    start_next = gs_ref[g_next]
    g_prev = jnp.maximum(g - 1, 0)
    n_prev = jnp.where(g > 0, gn_ref[g_prev], 0)
    start_prev = gs_ref[g_prev]

    def chunk_rows(s, c):
        return pl.ds(pl.multiple_of(s + c * EXPERT_CHUNK, EXPERT_CHUNK), EXPERT_CHUNK)

    def span(c, k=1):
        return slice(c * EXPERT_CHUNK, (c + k) * EXPERT_CHUNK)

    def x_copy(s, c):
        return pltpu.make_async_copy(xs_hbm.at[chunk_rows(s, c)], xland.at[c], sem_in.at[c])

    def y_copy(s, c):
        return pltpu.make_async_copy(ybuf.at[span(c)], os_hbm.at[chunk_rows(s, c)], sem_out.at[c])

    def for_chunks(cnt, fn, first=0):
        for c in range(first, GROUP_CHUNKS):
            pl.when(c < cnt)(functools.partial(fn, c))

    def load_chunk(c):
        x_copy(start, c).wait()
        xb[span(c), :] = xland[c].astype(BF16)

    def init_chunk(c):
        ybuf[span(c), :] = jnp.broadcast_to(bd_ref[...], (EXPERT_CHUNK, ybuf.shape[1]))

    def wait_prev_rows():
        for_chunks(n_prev, lambda c: y_copy(start_prev, c).wait())

    def compute_span(c, k, cast_weights=False):
        x = xb[span(c, k), :]
        if cast_weights:
            wgb[...] = wg_ref[...].astype(BF16)
        gt = jnp.minimum(_dot(x, wgb[...]) + bg_ref[...], SWIGLU_LIMIT)
        if cast_weights:
            wub[...] = wu_ref[...].astype(BF16)
        up = jnp.clip(_dot(x, wub[...]) + bu_ref[...], -SWIGLU_LIMIT, SWIGLU_LIMIT)
        act = ((up + 1.0) * gt * _sigmoid(SWIGLU_ALPHA * gt)).astype(BF16)
        if cast_weights:
            wdb[...] = wd_ref[...].astype(BF16)
        ybuf[span(c, k), :] += _dot(act, wdb[...])

    @pl.when((f == 0) & (g == 0))
    def _():
        for_chunks(n, lambda c: x_copy(start, c).start())

    @pl.when((f == 0) & (n == 0))
    def _():
        wait_prev_rows()

    @pl.when(n > 0)
    def _():
        @pl.when(f == 0)
        def _():
            for_chunks(n, load_chunk)
            wait_prev_rows()
            for_chunks(n, init_chunk)

        @pl.when(f == NF - 1)
        def _():
            for_chunks(n_next, lambda c: x_copy(start_next, c).start())

        for k in range(1, GROUP_CHUNKS + 1):
            pl.when(n == k)(functools.partial(compute_span, 0, k, True))

        @pl.when(f == NF - 1)
        def _():
            for_chunks(n, lambda c: y_copy(start, c).start())

            @pl.when(g == G - 1)
            def _():
                for_chunks(n, lambda c: y_copy(start, c).wait())


def _expert_call(l, xs, ge, gs, gn, w_gate, w_up, w_down, b_gate, b_up, b_down, tf):
    rows = xs.shape[0]
    D, F = w_gate.shape[-2:]
    NF = F // tf
    G = ge.shape[0]

    def fsel(f, gn, g):
        return jnp.where(gn[g] > 0, f, NF - 1)

    return pl.pallas_call(
        functools.partial(_expert_kernel, NF=NF, G=G),
        grid_spec=pltpu.PrefetchScalarGridSpec(
            num_scalar_prefetch=3,
            grid=(G, NF),
            in_specs=[
                pl.BlockSpec(memory_space=pl.ANY),
                pl.BlockSpec((None, None, D, tf), lambda g, f, ge, gs, gn: (l, ge[g], 0, fsel(f, gn, g))),
                pl.BlockSpec((None, None, D, tf), lambda g, f, ge, gs, gn: (l, ge[g], 0, fsel(f, gn, g))),
                pl.BlockSpec((None, None, tf, D), lambda g, f, ge, gs, gn: (l, ge[g], fsel(f, gn, g), 0)),
                pl.BlockSpec((None, None, 1, tf), lambda g, f, ge, gs, gn: (l, ge[g], 0, fsel(f, gn, g))),
                pl.BlockSpec((None, None, 1, tf), lambda g, f, ge, gs, gn: (l, ge[g], 0, fsel(f, gn, g))),
                pl.BlockSpec((None, None, 1, D), lambda g, f, ge, gs, gn: (l, ge[g], 0, 0)),
            ],
            out_specs=pl.BlockSpec(memory_space=pl.ANY),
            scratch_shapes=[
                pltpu.VMEM((GROUP_CHUNKS, EXPERT_CHUNK, xs.shape[1]), xs.dtype),
                pltpu.VMEM((GROUP_CHUNKS * EXPERT_CHUNK, D), BF16),
                pltpu.VMEM((GROUP_CHUNKS * EXPERT_CHUNK, D), F32),
                pltpu.VMEM((D, tf), BF16),
                pltpu.VMEM((D, tf), BF16),
                pltpu.VMEM((tf, D), BF16),
                pltpu.SemaphoreType.DMA((GROUP_CHUNKS,)),
                pltpu.SemaphoreType.DMA((GROUP_CHUNKS,)),
            ],
        ),
        out_shape=jax.ShapeDtypeStruct((rows, D), F32),
        compiler_params=_cp(("arbitrary", "arbitrary")),
        name="grouped_experts",
    )(ge, gs, gn, xs, w_gate, w_up, w_down, b_gate, b_up, b_down)


def _combine_kernel(dest_ref, os_hbm, x1_ref, g2_ref, route_ref, o_ref, buf, sem, *, tm):
    i = pl.program_id(0)
    nt = pl.num_programs(0)

    def issue(tile, slot):
        def body(r, c):
            t = tile * tm + r
            for k in range(TOP_K):
                pltpu.make_async_copy(os_hbm.at[pl.ds(dest_ref[t * TOP_K + k], 1)],
                                      buf.at[slot, k, pl.ds(r, 1)], sem.at[slot]).start(priority=k % 2)
            return c

        lax.fori_loop(0, tm, body, 0, unroll=8)

    @pl.when(i == 0)
    def _():
        issue(0, 0)

    @pl.when(i + 1 < nt)
    def _():
        issue(i + 1, (i + 1) % 2)

    slot = i % 2
    for k in range(TOP_K):
        pltpu.make_async_copy(os_hbm.at[pl.ds(0, tm)], buf.at[slot, k], sem.at[slot]).wait()
    route = route_ref[...]
    y = route[:, 2 * TOP_K:2 * TOP_K + 1] * buf[slot, 0]
    for k in range(1, TOP_K):
        y = y + route[:, 2 * TOP_K + k:2 * TOP_K + k + 1] * buf[slot, k]
    o_ref[...] = x1_ref[...] + g2_ref[...] * y


def _combine_call(dest, os_, x1, modt, route, tm):
    T, D = x1.shape
    return pl.pallas_call(
        functools.partial(_combine_kernel, tm=tm),
        grid_spec=pltpu.PrefetchScalarGridSpec(
            num_scalar_prefetch=1,
            grid=(T // tm,),
            in_specs=[
                pl.BlockSpec(memory_space=pl.ANY),
                pl.BlockSpec((tm, D), lambda i, d: (i, 0)),
                pl.BlockSpec((None, 1, D), lambda i, d: (i, 0, 5)),
                pl.BlockSpec((tm, LANES), lambda i, d: (i, 0)),
            ],
            out_specs=pl.BlockSpec((tm, D), lambda i, d: (i, 0)),
            scratch_shapes=[pltpu.VMEM((2, TOP_K, tm, D), F32), pltpu.SemaphoreType.DMA((2,))],
        ),
        out_shape=jax.ShapeDtypeStruct((T, D), F32),
        compiler_params=_cp(("arbitrary",)),
        name="expert_combine",
    )(dest, os_, x1, modt, route)


def _rope_tables(pos_a, pos_b, half, reps):
    freqs = ROPE_BASE ** (-jnp.arange(half, dtype=F32) / half)

    def blk(pos):
        ang = pos.astype(F32)[:, None] * freqs[None, :]
        c, s = jnp.cos(ang), jnp.sin(ang)
        return jnp.concatenate([c, c], -1), jnp.concatenate([-s, s], -1)

    ca, sa = blk(pos_a)
    cb, sb = blk(pos_b)
    cos = jnp.tile(jnp.concatenate([ca, cb], -1), (1, reps))
    sin = jnp.tile(jnp.concatenate([sa, sb], -1), (1, reps))
    return cos, sin


def _cond_tiles(T, Tp, Ss, tm):
    starts = np.arange(0, T, tm)
    return np.where(starts < Tp, 0, 1 + np.maximum(starts - Tp, 0) // Ss).astype(np.int32)


def _pick_tile(limit, *sizes):
    t = ROW_ALIGN
    while t * 2 <= limit and all(s % (t * 2) == 0 for s in sizes):
        t *= 2
    return t


def _route_plan(route, cnt, E, G):
    idx = route[:, 0:TOP_K].astype(jnp.int32)
    rank = route[:, TOP_K:2 * TOP_K].astype(jnp.int32)
    counts = cnt[0, :E].astype(jnp.int32)
    nch = (counts + EXPERT_CHUNK - 1) // EXPERT_CHUNK
    padded = nch * EXPERT_CHUNK
    base = jnp.cumsum(padded) - padded
    dest = (base[idx] + rank).reshape(-1)
    ngr = (nch + GROUP_CHUNKS - 1) // GROUP_CHUNKS
    gend = jnp.cumsum(ngr)
    gid = jnp.arange(G, dtype=jnp.int32)
    valid = gid < gend[-1]
    ge = jnp.minimum(jnp.sum((gid[:, None] >= gend[None, :]).astype(jnp.int32), axis=1), E - 1)
    j = gid - (gend - ngr)[ge]
    gs = base[ge] + j * (GROUP_CHUNKS * EXPERT_CHUNK)
    gn = jnp.clip(nch[ge] - j * GROUP_CHUNKS, 0, GROUP_CHUNKS)
    last = jnp.maximum(gend[-1] - 1, 0)
    ge = jnp.where(valid, ge, ge[last])
    gs = jnp.where(valid, gs, 0)
    gn = jnp.where(valid, gn, 0)
    return (dest.astype(jnp.int32), (base + counts).astype(jnp.int32), (padded - counts).astype(jnp.int32),
            ge.astype(jnp.int32), gs.astype(jnp.int32), gn.astype(jnp.int32))


def kernel(x_prompt, x_sample, cache_diff_k, cache_diff_v, cache_mla_ckv, cache_mla_krope, state_ret_fwd, state_ret_bwd, c, c_ctx, w_mod, b_mod, norm1, norm2, w_in, diff_q_gain, diff_k_gain, diff_lambda, diff_subln, mla_q_gain, w_uq, mla_kv_gain, w_ukv, mla_qn_gain, mla_kn_gain, ret_decay, w_br_diff, w_br_mla, w_br_ret, w_o, w_router, b_router, w_gate, b_gate, w_up, b_up, w_down, b_down):
    Bp, Sp, D = x_prompt.shape
    Bs, Ss, _ = x_sample.shape
    L = w_mod.shape[0]
    P = cache_diff_k.shape[2]
    E = w_router.shape[-1]
    F = w_gate.shape[-1]
    Tp, Ts = Bp * Sp, Bs * Ss
    T = Tp + Ts
    assert 1 + Bs <= COND_ROWS and E <= LANES
    assert Tp % ROW_ALIGN == 0 and Ss % ROW_ALIGN == 0 and Tp % Ss == 0 and Sp % RET_CHUNK == 0
    assert (3 * D) % 512 == 0 and F % 512 == 0

    off = 3 * D
    o_gl = sum((512, 512, 512, 512, 256, 64, 256, 256, 512, 512, 512))
    w_in_p = jnp.zeros((L, D, off + _SEC_TOTAL), BF16)
    for a, b, dst in ((o_gl, o_gl + off, 0), (0, 2304, off), (2368, 2880, off + _SEC["RQ"]),
                      (2304, 2368, off + _SEC["MKR"]), (2880, o_gl, off + _SEC["RV"])):
        w_in_p = lax.dynamic_update_slice(w_in_p, w_in[..., a:b].astype(BF16), (0, 0, dst))
    cond = jnp.zeros((COND_ROWS, D), F32).at[0].set(c_ctx).at[1:1 + Bs].set(c)
    wuq_p = jnp.pad(w_uq, ((0, 0), (0, 0), (0, 0), (0, MLA_NOPE + MLA_VD - MLA_QK))).reshape(
        L, MLA_Q_RANK, -1).astype(BF16)
    wukv_p = w_ukv.reshape(L, MLA_KV_RANK, -1).astype(BF16)
    pad_qk = ((0, 0), (0, MLA_NOPE + MLA_VD - MLA_QK))
    mla_w = (wuq_p, wukv_p, mla_q_gain[:, None, :], mla_kv_gain[:, None, :],
             jnp.pad(mla_qn_gain, pad_qk)[:, None, :], jnp.pad(mla_kn_gain, pad_qk)[:, None, :])
    diff_g = (jnp.tile(diff_q_gain, (1, 2))[:, None, :], jnp.tile(diff_k_gain, (1, 2))[:, None, :],
              diff_subln[:, None, :])
    wd_b, wm_b, wr_b, wo_b = (w.astype(BF16) for w in (w_br_diff, w_br_mla, w_br_ret, w_o))
    w_router_p = jnp.pad(w_router, ((0, 0), (0, 0), (0, LANES - E)))
    b_router_p = jnp.pad(b_router, ((0, 0), (0, LANES - E)))[:, None, :]
    lam_vec = diff_lambda.astype(F32)
    lam_init = jnp.asarray([0.8 - 0.6 * math.exp(-0.3 * l) for l in range(L)], F32)
    lam = (jnp.exp(jnp.sum(lam_vec[:, 0] * lam_vec[:, 1], -1))
           - jnp.exp(jnp.sum(lam_vec[:, 2] * lam_vec[:, 3], -1)) + lam_init)
    log_g = jax.nn.log_sigmoid(ret_decay.astype(F32)).reshape(-1)
    cdk = cache_diff_k.reshape(Bs, L, P, DIFF_HEADS * 2 * DIFF_HD)
    cdv = cache_diff_v.reshape(Bs, L, P, DIFF_HEADS * DIFF_VD)
    ckr = jnp.pad(cache_mla_krope, ((0, 0), (0, 0), (0, 0), (0, LANES - MLA_ROPE)))
    t_pos = jnp.arange(Ss, dtype=jnp.int32)
    tabs_ax = _rope_tables(t_pos // GRID_W, t_pos % GRID_W, 16, 2)
    tabs_1d = _rope_tables(t_pos, t_pos, 32, 1)
    bg4, bu4, bd4 = b_gate[:, :, None, :], b_up[:, :, None, :], b_down[:, :, None, :]

    tm_p = _pick_tile(1024, Tp, Ss)
    tn_p = 1536 if (off + _SEC_TOTAL) % 1536 == 0 else 512
    tm_r = ROW_ALIGN
    tq = ROW_ALIGN
    rows_sorted = T * TOP_K + E * EXPERT_CHUNK
    G = E + (T * TOP_K) // (GROUP_CHUNKS * EXPERT_CHUNK)
    tf = 512

    mod = _mod_call(cond, w_mod, b_mod)
    x = jnp.concatenate([x_prompt.reshape(Tp, D), x_sample.reshape(Ts, D)], axis=0)
    outs = [[] for _ in range(6)]
    for l in range(L):
        modt_p = mod[l][_cond_tiles(T, Tp, Ss, tm_p)][:, None, :]
        modt_r = mod[l][_cond_tiles(T, Tp, Ss, tm_r)][:, None, :]
        proj = _proj_call(l, x, modt_p, norm1[:, None, :], w_in_p, tm_p, tn_p)

        yd_p, kn_p = _diff_call(l, proj, off, lam, diff_g, Bp, Sp, 0, min(tq, Sp), False)
        (yd_s,) = _diff_call(l, proj, off, lam, diff_g, Bs, Ss, Tp, tq, True, (cdk, cdv), tabs_ax)
        ym_p, ckv_p = _mla_call(l, proj, off, mla_w, Bp, Sp, 0, min(tq, Sp), False)
        (ym_s,) = _mla_call(l, proj, off, mla_w, Bs, Ss, Tp, tq, True, (cache_mla_ckv, ckr), tabs_ax)
        yr_p, sf_p, sb_p = _ret_call(l, proj, off, log_g, Bp, Sp, 0, False)
        (yr_s,) = _ret_call(l, proj, off, log_g, Bs, Ss, Tp, True, (state_ret_fwd, state_ret_bwd), tabs_1d)

        yd = jnp.concatenate([yd_p, yd_s], axis=0)
        ym = jnp.concatenate([ym_p, ym_s], axis=0)
        yr = jnp.concatenate([yr_p, yr_s], axis=0)
        merged = _merge_call(l, yd, ym, yr, proj, wd_b, wm_b, wr_b, tm_r)
        x1, h2, route, cnt = _post_call(l, merged, x, wo_b, modt_r, norm2[:, None, :],
                                        w_router_p, b_router_p, E, tm_r)
        dest, pstart, pcnt, ge, gs, gn = _route_plan(route, cnt, E, G)
        xs = _dispatch_call(h2, dest, pstart, pcnt, rows_sorted, E, tm_r)
        os_ = _expert_call(l, xs, ge, gs, gn, w_gate, w_up, w_down, bg4, bu4, bd4, tf)
        x = _combine_call(dest, os_, x1, modt_r, route, tm_r)

        dv = proj[:Tp, off + _SEC["DV"]:off + _SEC["DV"] + DIFF_HEADS * DIFF_VD]
        kr = proj[:Tp, off + _SEC["MKR"]:off + _SEC["MKR"] + MLA_ROPE]
        for lst, v in zip(outs, (kn_p.reshape(Bp, Sp, DIFF_HEADS, 2 * DIFF_HD),
                                 dv.reshape(Bp, Sp, DIFF_HEADS, DIFF_VD),
                                 ckv_p.reshape(Bp, Sp, MLA_KV_RANK), kr.reshape(Bp, Sp, MLA_ROPE),
                                 sf_p, sb_p)):
            lst.append(v)

    y_prompt = x[:Tp].reshape(Bp, Sp, D)
    y_sample = x[Tp:].reshape(Bs, Ss, D)
    return (y_prompt, y_sample) + tuple(jnp.stack(o, axis=1) for o in outs)
```
